```python
import math
import jax
import jax.numpy as jnp
from jax import lax
import numpy as np

D_MODEL = 1024
BATCH = 2
SEQ = 8192
DEPTH = 2
DEC_BATCH = 32
DEC_SEQ = 4
PAST_LEN = 8192
PAGE_SIZE = 128

N_EVEN = (DEPTH + 1) // 2
N_ODD = DEPTH // 2
EPS = 1e-6
POOL_WINDOWS = (2, 4, 8, 16)
N_POOL_GROUPS = 4
POOL_GROUP_DIM = D_MODEL // 8
D_POOL = N_POOL_GROUPS * POOL_GROUP_DIM
POOL_STATE = max(POOL_WINDOWS) - 1
N_HEADS_B = 8
HEAD_DIM_B = 64
D_ATTN = N_HEADS_B * HEAD_DIM_B
N_IDX_HEADS = 8
IDX_DIM = 64
TOPK_MAX = 256
Q_BLOCK = 128
IN_SPLITS = (D_POOL, D_ATTN, D_ATTN, D_ATTN, N_IDX_HEADS * IDX_DIM, IDX_DIM, N_IDX_HEADS)
IN_COLS = sum(IN_SPLITS)
D_MIX = D_POOL + D_ATTN
D_CONV = D_MODEL
CONV_WIDTH = 31
CONV_STATE = CONV_WIDTH - 1
N_MEM = 256
N_HEADS_X = 4
HEAD_DIM_X = D_MODEL // N_HEADS_X
D_CROSS = N_HEADS_X * HEAD_DIM_X
N_GROUPS = 4
EXPERTS_PER_GROUP = 8
N_EXPERTS = N_GROUPS * EXPERTS_PER_GROUP
TOP_K_INNER = 2
EXPERT_FF = 512
MOE_BLOCK = 128

kernel_name = 'hybrid_pool_dsa_conformer_hmoe_step'


def rmsnorm(x, g):
    xf = x.astype(jnp.float32)
    y = xf * lax.rsqrt(jnp.mean(xf * xf, axis=-1, keepdims=True) + EPS)
    return (y * g.astype(jnp.float32)).astype(x.dtype)


def gather_rows(a, idx):
    return jax.vmap(lambda ab, ib: ab[ib])(a, idx)


def pool_mixer(u, prefix, pos0, w_pool, pool_scale):
    B, T, _ = u.shape
    ext = jnp.concatenate([prefix.astype(u.dtype), u], axis=1)
    cs = jnp.cumsum(ext.astype(jnp.float32), axis=1)
    cs = jnp.concatenate([jnp.zeros((B, 1, D_POOL), jnp.float32), cs], axis=1)
    end = cs[:, POOL_STATE + 1:]
    pos = pos0 + jnp.arange(T)
    diffs = []
    for gi, w in enumerate(POOL_WINDOWS):
        c0, c1 = gi * POOL_GROUP_DIM, (gi + 1) * POOL_GROUP_DIM
        start = cs[:, POOL_STATE + 1 - w: POOL_STATE + 1 - w + T, c0:c1]
        cnt = jnp.minimum(pos + 1, w).astype(jnp.float32)[None, :, None]
        diffs.append((end[..., c0:c1] - start) / cnt - u[..., c0:c1].astype(jnp.float32))
    d = jnp.stack(diffs, axis=2).astype(u.dtype)
    y = jnp.einsum('btgc,gcd->btgd', d, w_pool).reshape(B, T, D_POOL)
    return y * pool_scale, ext[:, -POOL_STATE:]


def indexer_select(qi, wi, ki, qpos, n_top):
    logits = jnp.einsum('bqhd,bld->bqhl', qi, ki).astype(jnp.float32)
    score = jnp.einsum('bqh,bqhl->bql', wi.astype(jnp.float32), jax.nn.relu(logits))
    kpos = jnp.arange(ki.shape[1])
    score = jnp.where(kpos[None, None, :] <= qpos[None, :, None], score, -jnp.inf)
    _, sel = lax.top_k(score, n_top)
    valid = sel <= qpos[None, :, None]
    return sel, valid


def sparse_attend(q, k_sel, v_sel, valid):
    s = jnp.einsum('bqhd,bqkhd->bhqk', q, k_sel).astype(jnp.float32) / math.sqrt(HEAD_DIM_B)
    s = jnp.where(valid[:, None], s, -jnp.inf)
    p = jax.nn.softmax(s, axis=-1).astype(v_sel.dtype)
    return jnp.einsum('bhqk,bqkhd->bqhd', p, v_sel)


def dsa_prompt(q, k, v, qi, wi, ki):
    B, T = q.shape[:2]
    n_top = min(TOPK_MAX, T // 4)
    nblk = T // Q_BLOCK

    def split(a):
        return a.reshape((B, nblk, Q_BLOCK) + a.shape[2:]).swapaxes(0, 1)

    def block(args):
        qb, qib, wib, pos = args
        sel, valid = indexer_select(qib, wib, ki, pos, n_top)
        return sparse_attend(qb, gather_rows(k, sel), gather_rows(v, sel), valid)

    pos = jnp.arange(T).reshape(nblk, Q_BLOCK)
    o = lax.map(block, (split(q), split(qi), split(wi), pos))
    return o.swapaxes(0, 1).reshape(B, T, N_HEADS_B, HEAD_DIM_B)


def dsa_sample(q, k_new, v_new, qi, wi, ki_new, pool_k, pool_v, pool_ki, page_table):
    DB, T = q.shape[:2]
    ki_past = pool_ki[page_table].reshape(DB, PAST_LEN, IDX_DIM)
    ki_all = jnp.concatenate([ki_past, ki_new], axis=1)
    qpos = PAST_LEN + jnp.arange(T)
    sel, valid = indexer_select(qi, wi, ki_all, qpos, min(TOPK_MAX, (PAST_LEN + T) // 4))
    s_past = jnp.minimum(sel, PAST_LEN - 1)
    phys = jax.vmap(lambda pt, s: pt[s // PAGE_SIZE])(page_table, s_past)
    row = s_past % PAGE_SIZE
    s_new = jnp.clip(sel - PAST_LEN, 0, T - 1)
    is_new = (sel >= PAST_LEN)[..., None, None]
    k_sel = jnp.where(is_new, gather_rows(k_new, s_new), pool_k[phys, row])
    v_sel = jnp.where(is_new, gather_rows(v_new, s_new), pool_v[phys, row])
    return sparse_attend(q, k_sel, v_sel, valid)


def even_mixer(h, pool_prefix, pos0, attn_fn, le, p):
    B, T, _ = h.shape
    z = h @ p['w_in_even'][le]
    o = [0]
    for n in IN_SPLITS:
        o.append(o[-1] + n)
    u = z[..., o[0]:o[1]]
    q = z[..., o[1]:o[2]].reshape(B, T, N_HEADS_B, HEAD_DIM_B)
    k = z[..., o[2]:o[3]].reshape(B, T, N_HEADS_B, HEAD_DIM_B)
    v = z[..., o[3]:o[4]].reshape(B, T, N_HEADS_B, HEAD_DIM_B)
    qi = z[..., o[4]:o[5]].reshape(B, T, N_IDX_HEADS, IDX_DIM)
    ki = z[..., o[5]:o[6]]
    wi = z[..., o[6]:o[7]] * (N_IDX_HEADS ** -0.5)
    a_out, new_pool = pool_mixer(u, pool_prefix, pos0, p['w_pool'][le], p['pool_scale'][le])
    b_out = attn_fn(le, q, k, v, qi, wi, ki).reshape(B, T, D_ATTN)
    y = jnp.concatenate([a_out, b_out], axis=-1) @ p['w_out_even'][le]
    return y, new_pool, k, v, ki


def conv_mixer(h, prefix, w_in, b_in, w_dw, b_dw, ln_g, ln_b, w_out):
    a = h @ w_in + b_in
    u = a[..., :D_CONV] * jax.nn.sigmoid(a[..., D_CONV:])
    ext = jnp.concatenate([prefix.astype(u.dtype), u], axis=1)
    y = lax.conv_general_dilated(ext, w_dw[:, None, :], window_strides=(1,), padding='VALID',
                                 dimension_numbers=('NWC', 'WIO', 'NWC'),
                                 feature_group_count=D_CONV) + b_dw
    yf = y.astype(jnp.float32)
    mu = jnp.mean(yf, axis=-1, keepdims=True)
    var = jnp.mean(jnp.square(yf - mu), axis=-1, keepdims=True)
    yn = ((yf - mu) * lax.rsqrt(var + EPS) * ln_g.astype(jnp.float32) + ln_b.astype(jnp.float32)).astype(h.dtype)
    return jax.nn.silu(yn) @ w_out, ext[:, -CONV_STATE:]


def mem_kv(mem, g_mem, w_ck, w_cv):
    B = mem.shape[0]
    m = rmsnorm(mem, g_mem)
    k = (m @ w_ck).reshape(B, N_MEM, N_HEADS_X, HEAD_DIM_X)
    v = (m @ w_cv).reshape(B, N_MEM, N_HEADS_X, HEAD_DIM_X)
    return k, v


def cross_attend(h, k, v, w_cq, w_co):
    B, T, _ = h.shape
    q = (h @ w_cq).reshape(B, T, N_HEADS_X, HEAD_DIM_X)
    s = jnp.einsum('bthd,bmhd->bhtm', q, k).astype(jnp.float32) / math.sqrt(HEAD_DIM_X)
    pr = jax.nn.softmax(s, axis=-1).astype(v.dtype)
    o = jnp.einsum('bhtm,bmhd->bthd', pr, v).reshape(B, T, D_CROSS)
    return o @ w_co


def route(h2, w_rg, b_rg, w_re, b_re):
    g_logits = (h2 @ w_rg).astype(jnp.float32) + b_rg.astype(jnp.float32)
    g_prob = jax.nn.softmax(g_logits, axis=-1)
    g_sel = jnp.argmax(g_logits, axis=-1)
    e_all = jnp.einsum('nd,gde->nge', h2, w_re).astype(jnp.float32) + b_re.astype(jnp.float32)
    e_logits = jnp.take_along_axis(e_all, g_sel[:, None, None], axis=1)[:, 0]
    top_val, top_idx = lax.top_k(e_logits, TOP_K_INNER)
    gates = jax.nn.softmax(top_val, axis=-1) * jnp.take_along_axis(g_prob, g_sel[:, None], axis=1)
    return g_sel[:, None] * EXPERTS_PER_GROUP + top_idx, gates


def moe_dispatch(h2, expert_ids, gates, w1, w3, w2):
    N, D = h2.shape
    A = N * TOP_K_INNER
    eid = expert_ids.reshape(A)
    tok = jnp.repeat(jnp.arange(N, dtype=jnp.int32), TOP_K_INNER)
    gate = gates.reshape(A).astype(h2.dtype)
    order = jnp.argsort(eid)
    eid_s, tok_s, gate_s = eid[order], tok[order], gate[order]
    counts = jnp.bincount(eid, length=N_EXPERTS)
    start = jnp.cumsum(counts) - counts
    pcounts = (counts + MOE_BLOCK - 1) // MOE_BLOCK * MOE_BLOCK
    pend = jnp.cumsum(pcounts)
    pstart = pend - pcounts
    slot = pstart[eid_s] + (jnp.arange(A) - start[eid_s])
    n_blocks = -(-(A + N_EXPERTS * (MOE_BLOCK - 1)) // MOE_BLOCK)
    P = n_blocks * MOE_BLOCK
    buf_tok = jnp.zeros((P,), jnp.int32).at[slot].set(tok_s)
    buf_gate = jnp.zeros((P,), h2.dtype).at[slot].set(gate_s)
    blk_exp = jnp.minimum(jnp.searchsorted(pend, jnp.arange(n_blocks) * MOE_BLOCK, side='right'), N_EXPERTS - 1)

    def run(args):
        toks, e = args
        xb = h2[toks]
        return (jax.nn.silu(xb @ w1[e]) * (xb @ w3[e])) @ w2[e]

    yb = lax.map(run, (buf_tok.reshape(n_blocks, MOE_BLOCK), blk_exp)).reshape(P, D)
    return jnp.zeros_like(h2).at[buf_tok].add(yb * buf_gate[:, None])


def moe(h, w_rg, b_rg, w_re, b_re, w1, w3, w2):
    B, T, D = h.shape
    h2 = h.reshape(B * T, D)
    eids, gates = route(h2, w_rg, b_rg, w_re, b_re)
    return moe_dispatch(h2, eids, gates, w1, w3, w2).reshape(B, T, D)


def trunk(x, pos0, pool_prefix, conv_prefix, mem_k, mem_v, attn_fn, p):
    new_k, new_v, new_ki, new_pool, new_conv = [], [], [], [], []
    for l in range(DEPTH):
        h = rmsnorm(x, p['g_mix'][l])
        if l % 2 == 0:
            le = l // 2
            y, st, k, v, ki = even_mixer(h, pool_prefix[le], pos0, attn_fn, le, p)
            new_pool.append(st)
            new_k.append(k)
            new_v.append(v)
            new_ki.append(ki)
        else:
            lo = l // 2
            y, st = conv_mixer(h, conv_prefix[lo], p['w_conv_in'][lo], p['b_conv_in'][lo], p['w_dw'][lo],
                               p['b_dw'][lo], p['ln_g'][lo], p['ln_b'][lo], p['w_conv_out'][lo])
            new_conv.append(st)
        x = x + y
        x = x + cross_attend(rmsnorm(x, p['g_cross'][l]), mem_k[l], mem_v[l], p['w_cq'][l], p['w_co'][l])
        x = x + moe(rmsnorm(x, p['g_ffn'][l]), p['w_rg'][l], p['b_rg'][l], p['w_re'][l], p['b_re'][l],
                    p['w_e1'][l], p['w_e3'][l], p['w_e2'][l])
    y = rmsnorm(x, p['g_final'])
    return y, jnp.stack(new_k), jnp.stack(new_v), jnp.stack(new_ki), jnp.stack(new_pool), jnp.stack(new_conv)


def setup_inputs(seed: int = 0) -> dict:
    key = jax.random.key(seed)
    ks = list(jax.random.split(key, 48))

    def nrm(shape, scale=1.0):
        return jax.random.normal(ks.pop(), shape, jnp.float32) * scale

    def gain(shape):
        return 1.0 + nrm(shape, 0.1)

    n_pages = PAST_LEN // PAGE_SIZE
    n_used = DEC_BATCH * n_pages
    n_pool = (5 * n_used + 3) // 4
    perm = jax.random.permutation(ks.pop(), n_pool)
    page_table = perm[:n_used].reshape(DEC_BATCH, n_pages).astype(jnp.int32)
    return {
        'x_prompt': nrm((BATCH, SEQ, D_MODEL)),
        'x_sample': nrm((DEC_BATCH, DEC_SEQ, D_MODEL)),
        'mem_prompt': nrm((BATCH, N_MEM, D_MODEL)),
        'cache_attn_k': nrm((N_EVEN, n_pool, PAGE_SIZE, N_HEADS_B, HEAD_DIM_B)),
        'cache_attn_v': nrm((N_EVEN, n_pool, PAGE_SIZE, N_HEADS_B, HEAD_DIM_B)),
        'cache_idx_k': nrm((N_EVEN, n_pool, PAGE_SIZE, IDX_DIM)),
        'page_table': page_table,
        'state_pool': nrm((N_EVEN, DEC_BATCH, POOL_STATE, D_POOL)),
        'state_conv': nrm((N_ODD, DEC_BATCH, CONV_STATE, D_CONV), 0.5),
        'cache_mem_k': nrm((DEPTH, DEC_BATCH, N_MEM, N_HEADS_X, HEAD_DIM_X)),
        'cache_mem_v': nrm((DEPTH, DEC_BATCH, N_MEM, N_HEADS_X, HEAD_DIM_X)),
        'g_mix': gain((DEPTH, D_MODEL)),
        'g_cross': gain((DEPTH, D_MODEL)),
        'g_mem': gain((DEPTH, D_MODEL)),
        'g_ffn': gain((DEPTH, D_MODEL)),
        'g_final': gain((D_MODEL,)),
        'w_in_even': nrm((N_EVEN, D_MODEL, IN_COLS), D_MODEL ** -0.5),
        'w_pool': nrm((N_EVEN, N_POOL_GROUPS, POOL_GROUP_DIM, POOL_GROUP_DIM), POOL_GROUP_DIM ** -0.5),
        'pool_scale': gain((N_EVEN, D_POOL)),
        'w_out_even': nrm((N_EVEN, D_MIX, D_MODEL), D_MIX ** -0.5),
        'w_conv_in': nrm((N_ODD, D_MODEL, 2 * D_CONV), D_MODEL ** -0.5),
        'b_conv_in': nrm((N_ODD, 2 * D_CONV), 0.02),
        'w_dw': nrm((N_ODD, CONV_WIDTH, D_CONV), CONV_WIDTH ** -0.5),
        'b_dw': nrm((N_ODD, D_CONV), 0.02),
        'ln_g': gain((N_ODD, D_CONV)),
        'ln_b': nrm((N_ODD, D_CONV), 0.02),
        'w_conv_out': nrm((N_ODD, D_CONV, D_MODEL), D_CONV ** -0.5),
        'w_cq': nrm((DEPTH, D_MODEL, D_CROSS), D_MODEL ** -0.5),
        'w_ck': nrm((DEPTH, D_MODEL, D_CROSS), D_MODEL ** -0.5),
        'w_cv': nrm((DEPTH, D_MODEL, D_CROSS), D_MODEL ** -0.5),
        'w_co': nrm((DEPTH, D_CROSS, D_MODEL), D_CROSS ** -0.5),
        'w_rg': nrm((DEPTH, D_MODEL, N_GROUPS), D_MODEL ** -0.5),
        'b_rg': nrm((DEPTH, N_GROUPS), 0.01),
        'w_re': nrm((DEPTH, N_GROUPS, D_MODEL, EXPERTS_PER_GROUP), D_MODEL ** -0.5),
        'b_re': nrm((DEPTH, N_GROUPS, EXPERTS_PER_GROUP), 0.01),
        'w_e1': nrm((DEPTH, N_EXPERTS, D_MODEL, EXPERT_FF), D_MODEL ** -0.5),
        'w_e3': nrm((DEPTH, N_EXPERTS, D_MODEL, EXPERT_FF), D_MODEL ** -0.5),
        'w_e2': nrm((DEPTH, N_EXPERTS, EXPERT_FF, D_MODEL), EXPERT_FF ** -0.5),
    }


def reference(x_prompt, x_sample, mem_prompt, cache_attn_k, cache_attn_v, cache_idx_k, page_table,
              state_pool, state_conv, cache_mem_k, cache_mem_v, g_mix, g_cross, g_mem, g_ffn, g_final,
              w_in_even, w_pool, pool_scale, w_out_even, w_conv_in, b_conv_in, w_dw, b_dw, ln_g, ln_b,
              w_conv_out, w_cq, w_ck, w_cv, w_co, w_rg, b_rg, w_re, b_re, w_e1, w_e3, w_e2):
    p = {'g_mix': g_mix, 'g_cross': g_cross, 'g_ffn': g_ffn, 'g_final': g_final,
         'w_in_even': w_in_even, 'w_pool': w_pool, 'pool_scale': pool_scale, 'w_out_even': w_out_even,
         'w_conv_in': w_conv_in, 'b_conv_in': b_conv_in, 'w_dw': w_dw, 'b_dw': b_dw, 'ln_g': ln_g,
         'ln_b': ln_b, 'w_conv_out': w_conv_out, 'w_cq': w_cq, 'w_co': w_co, 'w_rg': w_rg, 'b_rg': b_rg,
         'w_re': w_re, 'b_re': b_re, 'w_e1': w_e1, 'w_e3': w_e3, 'w_e2': w_e2}

    mem_k_p, mem_v_p = [], []
    for l in range(DEPTH):
        mk, mv = mem_kv(mem_prompt, g_mem[l], w_ck[l], w_cv[l])
        mem_k_p.append(mk)
        mem_v_p.append(mv)
    pool0 = jnp.zeros((N_EVEN, BATCH, POOL_STATE, D_POOL), x_prompt.dtype)
    conv0 = jnp.zeros((N_ODD, BATCH, CONV_STATE, D_CONV), x_prompt.dtype)

    def prompt_attn(le, q, k, v, qi, wi, ki):
        return dsa_prompt(q, k, v, qi, wi, ki)

    y_prompt, nk_p, nv_p, nki_p, npool_p, nconv_p = trunk(
        x_prompt, 0, pool0, conv0, mem_k_p, mem_v_p, prompt_attn, p)
    new_mem_k_prompt = jnp.stack(mem_k_p)
    new_mem_v_prompt = jnp.stack(mem_v_p)

    def sample_attn(le, q, k, v, qi, wi, ki):
        return dsa_sample(q, k, v, qi, wi, ki, cache_attn_k[le], cache_attn_v[le], cache_idx_k[le], page_table)

    y_sample, nk_s, nv_s, nki_s, npool_s, nconv_s = trunk(
        x_sample, PAST_LEN, state_pool, state_conv, cache_mem_k, cache_mem_v, sample_attn, p)

    return (y_prompt, y_sample, nk_p, nv_p, nki_p, npool_p, nconv_p, new_mem_k_prompt, new_mem_v_prompt,
            nk_s, nv_s, nki_s, npool_s, nconv_s)
```

```python
import functools
import math

import jax
import jax.numpy as jnp
from jax import lax
from jax.experimental import pallas as pl
from jax.experimental.pallas import tpu as pltpu

F32 = jnp.float32
BF16 = jnp.bfloat16
I32 = jnp.int32

EPS = 1e-6
POOL_WINDOWS = (2, 4, 8, 16)
POOL_HIST = 16
CONV_WIDTH = 31
CONV_HIST = 32
N_HEADS_B = 8
HEAD_DIM_B = 64
N_IDX_HEADS = 8
IDX_DIM = 64
TOPK_MAX = 256
Q_BLOCK = 128
KEY_CHUNK = 256
N_HEADS_X = 4
N_GROUPS = 4
EXPERTS_PER_GROUP = 8
N_EXPERTS = N_GROUPS * EXPERTS_PER_GROUP
LANES = 128
INT_MIN = -(2 ** 31)
NEG_BIG = -1e30
MIB = 1024 * 1024

_NT = (((1,), (1,)), ((), ()))


def _params(n_axes, vmem_mib=None, **kw):
    if vmem_mib is not None:
        kw["vmem_limit_bytes"] = vmem_mib * MIB
    return pltpu.CompilerParams(dimension_semantics=("arbitrary",) * n_axes, **kw)


def _rms(x, g):
    return x * lax.rsqrt(jnp.mean(x * x, axis=-1, keepdims=True) + EPS) * g


def _sigmoid(x):
    return 1.0 / (1.0 + jnp.exp(-x))


def _sortable_key(score):
    bits = pltpu.bitcast(score, I32)
    return bits ^ ((bits >> 31) & 0x7FFFFFFF)


def _token_tile(n):
    return 512 if n % 512 == 0 else n


def _inproj_body(x_ref, g_ref, w_ref, *refs, with_vt):
    if with_vt:
        wvt_ref, refs = refs[0], refs[1:]
    u_ref, q_ref, k_ref, v_ref, kb_ref, qi_ref, ki_ref, kib_ref, wi_ref = refs[:9]
    hb = _rms(x_ref[...], g_ref[...]).astype(BF16)

    def proj(c0, c1):
        return jnp.dot(hb, w_ref[:, c0:c1], preferred_element_type=F32)

    u_ref[...] = proj(0, 512)
    q_ref[...] = proj(512, 1024).astype(BF16)
    k = proj(1024, 1536)
    k_ref[...] = k
    kb_ref[...] = k.astype(BF16)
    v_ref[...] = proj(1536, 2048)
    qi_ref[...] = proj(2048, 2560).astype(BF16)
    ki = proj(2560, 2688)[:, :IDX_DIM]
    ki_ref[...] = ki
    kib_ref[...] = ki.astype(BF16)
    wi_ref[...] = proj(2688, 2816)[:, :N_IDX_HEADS] * (N_IDX_HEADS ** -0.5)
    if with_vt:
        vt_ref = refs[9]
        vt = lax.dot_general(wvt_ref[...], hb, _NT, preferred_element_type=F32)
        for c in range(vt_ref.shape[0]):
            vt_ref[c] = vt[:, c * KEY_CHUNK:(c + 1) * KEY_CHUNK].astype(BF16)


def _pack_w_in(w_in):
    d = w_in.shape[0]
    main = w_in[:, :2560]
    ki = jnp.pad(w_in[:, 2560:2624], ((0, 0), (0, 64)))
    wi = jnp.pad(w_in[:, 2624:2632], ((0, 0), (0, 120)))
    del d
    return jnp.concatenate([main, ki, wi], axis=1).astype(BF16)


def _inproj(x2d, g, w_in, with_vt):
    n, d = x2d.shape
    tm = _token_tile(n)
    w = _pack_w_in(w_in)
    row = lambda i: (i, 0)
    fixed = lambda i: (0, 0)
    in_specs = [pl.BlockSpec((tm, d), row), pl.BlockSpec((1, d), fixed), pl.BlockSpec(w.shape, fixed)]
    args = [x2d, g.reshape(1, d), w]
    if with_vt:
        wvt = w_in[:, 1536:2048].T.astype(BF16)
        in_specs.append(pl.BlockSpec(wvt.shape, fixed))
        args.append(wvt)
    shapes = [((n, 512), F32), ((n, 512), BF16), ((n, 512), F32), ((n, 512), F32), ((n, 512), BF16),
              ((n, 512), BF16), ((n, IDX_DIM), F32), ((n, IDX_DIM), BF16), ((n, N_IDX_HEADS), F32)]
    out_shape = [jax.ShapeDtypeStruct(s, t) for s, t in shapes]
    out_specs = [pl.BlockSpec((tm, s[1]), row) for s, _ in shapes]
    if with_vt:
        cpt = tm // KEY_CHUNK
        out_shape.append(jax.ShapeDtypeStruct((n // KEY_CHUNK, 512, KEY_CHUNK), BF16))
        out_specs.append(pl.BlockSpec((cpt, 512, KEY_CHUNK), lambda i: (i, 0, 0)))
    return pl.pallas_call(
        functools.partial(_inproj_body, with_vt=with_vt),
        grid=(n // tm,), in_specs=in_specs, out_specs=out_specs, out_shape=out_shape,
        compiler_params=_params(1, 48), name="inproj_even")(*args)


def _dsa_prompt_body(qi_ref, wit_ref, ki_ref, q_ref, kb_ref, vt_ref, o_ref,
                     skey, bias, acc, m_s, l_s, *, topk):
    ck, qb = KEY_CHUNK, Q_BLOCK
    i = pl.program_id(1)
    n_keys = (i + 1) * qb
    nch = (n_keys + ck - 1) // ck

    qis = [qi_ref[:, h * IDX_DIM:(h + 1) * IDX_DIM] for h in range(N_IDX_HEADS)]
    wit = wit_ref[...]
    q_pos = i * qb + lax.broadcasted_iota(I32, (ck, qb), 1)
    k_row = lax.broadcasted_iota(I32, (ck, qb), 0)

    def score_chunk(c, carry):
        off = pl.multiple_of(c * ck, ck)
        kc = ki_ref[pl.ds(off, ck), :]
        sc = jnp.zeros((ck, qb), F32)
        for h in range(N_IDX_HEADS):
            lg = lax.dot_general(kc, qis[h], _NT, preferred_element_type=F32)
            sc = sc + jnp.maximum(lg, 0.0) * wit[h:h + 1, :]
        key = jnp.where(k_row + off <= q_pos, _sortable_key(sc), INT_MIN)
        skey[pl.ds(off, ck), :] = key
        return carry

    lax.fori_loop(0, nch, score_chunk, 0)

    def count_ge(tv):
        def body(c, cnt):
            off = pl.multiple_of(c * ck, ck)
            ind = jnp.where(skey[pl.ds(off, ck), :] >= tv, 1.0, 0.0)
            return cnt + jnp.sum(ind.reshape(ck // 8, 8, qb), axis=0)
        cnt8 = lax.fori_loop(0, nch, body, jnp.zeros((8, qb), F32))
        return jnp.sum(cnt8, axis=0, keepdims=True)

    def bit_step(it, carry):
        prefix, n_ge = carry
        cand = prefix | (jnp.int32(1) << (31 - it))
        cnt = count_ge(cand ^ INT_MIN)
        ok = cnt >= topk
        return jnp.where(ok, cand, prefix), jnp.where(ok, cnt, n_ge)

    prefix, n_ge = lax.fori_loop(
        0, 32, bit_step, (jnp.zeros((1, qb), I32), jnp.full((1, qb), 2.0 ** 30, F32)))
    thr = prefix ^ INT_MIN
    has_k = thr > INT_MIN
    tie_excess = jnp.max(jnp.where(has_k & (n_ge > topk), 1.0, 0.0)) > 0.5

    @pl.when(jnp.logical_not(tie_excess))
    def _():
        tsel = jnp.maximum(thr, INT_MIN + 1)

        def body(c, carry):
            off = pl.multiple_of(c * ck, ck)
            bias[pl.ds(off, ck), :] = jnp.where(skey[pl.ds(off, ck), :] >= tsel, 0.0, NEG_BIG)
            return carry
        lax.fori_loop(0, nch, body, 0)

    @pl.when(tie_excess)
    def _():
        n_gt = count_ge(thr + 1)
        need = topk - n_gt
        r = lax.broadcasted_iota(I32, (ck, ck), 0)
        cidx = lax.broadcasted_iota(I32, (ck, ck), 1)
        tri = jnp.where(cidx <= r, 1.0, 0.0).astype(BF16)

        def body(c, run):
            off = pl.multiple_of(c * ck, ck)
            key = skey[pl.ds(off, ck), :]
            eq = jnp.where(key == thr, 1.0, 0.0)
            incl = jnp.dot(tri, eq.astype(BF16), preferred_element_type=F32)
            rank = incl - eq + run
            sel = (key > thr) | ((eq > 0.0) & (rank < need) & (key > INT_MIN))
            bias[pl.ds(off, ck), :] = jnp.where(sel, 0.0, NEG_BIG)
            return run + jnp.sum(eq, axis=0, keepdims=True)
        lax.fori_loop(0, nch, body, jnp.zeros((1, qb), F32))

    qf = q_ref[...].astype(F32)
    lane = lax.broadcasted_iota(I32, (qb, LANES), 1)
    qp = []
    for h in range(N_HEADS_B):
        pair = qf[:, (h // 2) * LANES:(h // 2 + 1) * LANES]
        keep = (lane >= HEAD_DIM_B) if h % 2 else (lane < HEAD_DIM_B)
        qp.append((jnp.where(keep, pair, 0.0) * (HEAD_DIM_B ** -0.5)).astype(BF16))
    m_s[...] = jnp.full(m_s.shape, NEG_BIG, F32)
    l_s[...] = jnp.zeros(l_s.shape, F32)
    acc[...] = jnp.zeros(acc.shape, F32)

    def attend_chunk(c, carry):
        off = pl.multiple_of(c * ck, ck)
        b = bias[pl.ds(off, ck), :]
        for h in range(N_HEADS_B):
            k2 = kb_ref[pl.ds(off, ck), (h // 2) * LANES:(h // 2 + 1) * LANES]
            s = lax.dot_general(k2, qp[h], _NT, preferred_element_type=F32) + b
            m_old = m_s[h:h + 1, :]
            m_new = jnp.maximum(m_old, jnp.max(s, axis=0, keepdims=True))
            alpha = jnp.exp(m_old - m_new)
            p = jnp.exp(s - m_new)
            l_s[h:h + 1, :] = alpha * l_s[h:h + 1, :] + jnp.sum(p, axis=0, keepdims=True)
            m_s[h:h + 1, :] = m_new
            rows = slice(h * HEAD_DIM_B, (h + 1) * HEAD_DIM_B)
            pv = jnp.dot(vt_ref[c, rows, :], p.astype(BF16), preferred_element_type=F32)
            acc[rows, :] = acc[rows, :] * alpha + pv
        return carry

    lax.fori_loop(0, nch, attend_chunk, 0)
    for h in range(N_HEADS_B):
        rows = slice(h * HEAD_DIM_B, (h + 1) * HEAD_DIM_B)
        acc[rows, :] = acc[rows, :] / l_s[h:h + 1, :]
    o_ref[...] = acc[...].T.astype(BF16)


def _dsa_prompt(qi, wi, kib, q, kb, vt):
    bsz, t, _ = q.shape
    nblk = t // Q_BLOCK
    topk = min(TOPK_MAX, t // 4)
    wit = wi.reshape(bsz, nblk, Q_BLOCK, N_IDX_HEADS).swapaxes(2, 3)
    blk = lambda b, i: (b, i, 0)
    whole = lambda b, i: (b, 0, 0)
    once = pl.Buffered(1)
    return pl.pallas_call(
        functools.partial(_dsa_prompt_body, topk=topk),
        grid=(bsz, nblk),
        in_specs=[
            pl.BlockSpec((None, Q_BLOCK, 512), blk),
            pl.BlockSpec((None, None, N_IDX_HEADS, Q_BLOCK), lambda b, i: (b, i, 0, 0)),
            pl.BlockSpec((None, t, IDX_DIM), whole, pipeline_mode=once),
            pl.BlockSpec((None, Q_BLOCK, 512), blk),
            pl.BlockSpec((None, t, 512), whole, pipeline_mode=once),
            pl.BlockSpec((None, t // KEY_CHUNK, 512, KEY_CHUNK), lambda b, i: (b, 0, 0, 0),
                         pipeline_mode=once),
        ],
        out_specs=pl.BlockSpec((None, Q_BLOCK, 512), blk),
        out_shape=jax.ShapeDtypeStruct((bsz, t, 512), BF16),
        scratch_shapes=[pltpu.VMEM((t, Q_BLOCK), I32), pltpu.VMEM((t, Q_BLOCK), F32),
                        pltpu.VMEM((512, Q_BLOCK), F32), pltpu.VMEM((N_HEADS_B, Q_BLOCK), F32),
                        pltpu.VMEM((N_HEADS_B, Q_BLOCK), F32)],
        compiler_params=_params(2, 48), name="dsa_prompt")(qi, wit, kib, q, kb, vt)


def _dsa_sample_select_body(pt_ref, qih_ref, wib_ref, kpool_ref, knew_ref, bias_ref, skey,
                            *, topk, n_pages, t_new):
    del pt_ref
    j = pl.program_id(1)
    nj = n_pages + 1
    lane = lax.broadcasted_iota(I32, (8, LANES), 1)
    qrow = lax.broadcasted_iota(I32, (8, LANES), 0)

    def score(kpage):
        lg = lax.dot_general(qih_ref[...], kpage.astype(BF16), _NT, preferred_element_type=F32)
        r = jnp.maximum(lg, 0.0) * wib_ref[...]
        sc = r[0:8]
        for h in range(1, N_IDX_HEADS):
            sc = sc + r[h * 8:(h + 1) * 8]
        return _sortable_key(sc)

    @pl.when(j < n_pages)
    def _():
        skey[j] = score(kpool_ref[...])

    @pl.when(j == n_pages)
    def _():
        ok = (lane <= qrow) & (lane < t_new)
        skey[j] = jnp.where(ok, score(knew_ref[...]), INT_MIN)

        def count_ge(tv):
            def body(jj, cnt):
                return cnt + jnp.where(skey[jj] >= tv, 1.0, 0.0)
            cnt = lax.fori_loop(0, nj, body, jnp.zeros((8, LANES), F32))
            return jnp.sum(cnt, axis=1, keepdims=True)

        def bit_step(it, carry):
            prefix, n_ge = carry
            cand = prefix | (jnp.int32(1) << (31 - it))
            cnt = count_ge(cand ^ INT_MIN)
            good = cnt >= topk
            return jnp.where(good, cand, prefix), jnp.where(good, cnt, n_ge)

        prefix, n_ge = lax.fori_loop(
            0, 32, bit_step, (jnp.zeros((8, 1), I32), jnp.full((8, 1), 2.0 ** 30, F32)))
        thr = prefix ^ INT_MIN
        has_k = thr > INT_MIN
        tie_excess = jnp.max(jnp.where(has_k & (n_ge > topk), 1.0, 0.0)) > 0.5

        @pl.when(jnp.logical_not(tie_excess))
        def _():
            tsel = jnp.maximum(thr, INT_MIN + 1)

            def body(jj, carry):
                bias_ref[jj] = jnp.where(skey[jj] >= tsel, 0.0, NEG_BIG)
                return carry
            lax.fori_loop(0, nj, body, 0)

        @pl.when(tie_excess)
        def _():
            n_gt = count_ge(thr + 1)
            need = topk - n_gt
            a = lax.broadcasted_iota(I32, (LANES, LANES), 0)
            bcol = lax.broadcasted_iota(I32, (LANES, LANES), 1)
            upper = jnp.where(a <= bcol, 1.0, 0.0).astype(BF16)

            def body(jj, run):
                key = skey[jj]
                eq = jnp.where(key == thr, 1.0, 0.0)
                incl = jnp.dot(eq.astype(BF16), upper, preferred_element_type=F32)
                rank = incl - eq + run
                sel = (key > thr) | ((eq > 0.0) & (rank < need) & (key > INT_MIN))
                bias_ref[jj] = jnp.where(sel, 0.0, NEG_BIG)
                return run + jnp.sum(eq, axis=1, keepdims=True)
            lax.fori_loop(0, nj, body, jnp.zeros((8, 1), F32))


def _dsa_sample_attend_body(pt_ref, qbd_ref, bias_ref, kpool_ref, vpool_ref, knew_ref, vnew_ref,
                            o_ref, m_s, l_s, acc, *, n_pages):
    del pt_ref
    j = pl.program_id(1)

    @pl.when(j == 0)
    def _():
        m_s[...] = jnp.full(m_s.shape, NEG_BIG, F32)
        l_s[...] = jnp.zeros(l_s.shape, F32)
        acc[...] = jnp.zeros(acc.shape, F32)

    def step(k, v):
        s = lax.dot_general(qbd_ref[...], k.astype(BF16), _NT, preferred_element_type=F32)
        s = s + jnp.concatenate([bias_ref[...]] * N_HEADS_B, axis=0)
        m_old = m_s[...]
        m_new = jnp.maximum(m_old, jnp.max(s, axis=1, keepdims=True))
        alpha = jnp.exp(m_old - m_new)
        p = jnp.exp(s - m_new)
        l_s[...] = alpha * l_s[...] + jnp.sum(p, axis=1, keepdims=True)
        m_s[...] = m_new
        pv = jnp.dot(p.astype(BF16), v.astype(BF16), preferred_element_type=F32)
        acc[...] = acc[...] * alpha[:, 0:1] + pv

    @pl.when(j < n_pages)
    def _():
        step(kpool_ref[...], vpool_ref[...])

    @pl.when(j == n_pages)
    def _():
        step(knew_ref[...], vnew_ref[...])
        o_ref[...] = acc[...] / l_s[:, 0:1]


def _dsa_sample(q, k_new, v_new, qi, wi, ki_new, pool_k, pool_v, pool_ki, page_table):
    db, t, _ = q.shape
    n_pages = page_table.shape[1]
    page = pool_ki.shape[1]
    past = n_pages * page
    topk = min(TOPK_MAX, (past + t) // 4)
    n_pool = pool_k.shape[0]
    nj = n_pages + 1

    def head_rows(a):
        a = jnp.pad(a.swapaxes(1, 2), ((0, 0), (0, 0), (0, 8 - t), (0, 0)))
        return a.reshape(db, 64, a.shape[-1])

    qih = head_rows(qi.reshape(db, t, N_IDX_HEADS, IDX_DIM)).astype(BF16)
    wib = jnp.broadcast_to(head_rows(wi.reshape(db, t, N_IDX_HEADS, 1)), (db, 64, LANES))
    qh = head_rows(q.reshape(db, t, N_HEADS_B, HEAD_DIM_B) * (HEAD_DIM_B ** -0.5))
    head_of_row = jnp.arange(64) // 8
    onehot = (head_of_row[:, None] == jnp.arange(N_HEADS_B)[None, :]).astype(F32)
    qbd = (qh[:, :, None, :] * onehot[None, :, :, None]).reshape(db, 64, 512).astype(BF16)
    pad_rows = lambda a: jnp.pad(a, ((0, 0), (0, page - t), (0, 0)))
    kin, kn, vn = pad_rows(ki_new), pad_rows(k_new), pad_rows(v_new)

    seq = lambda b, j, pt: (b, 0, 0)
    pg = lambda b, j, pt: (pt[b, jnp.minimum(j, n_pages - 1)], 0, 0)
    bias = pl.pallas_call(
        functools.partial(_dsa_sample_select_body, topk=topk, n_pages=n_pages, t_new=t),
        grid_spec=pltpu.PrefetchScalarGridSpec(
            num_scalar_prefetch=1, grid=(db, nj),
            in_specs=[pl.BlockSpec((None, 64, IDX_DIM), seq), pl.BlockSpec((None, 64, LANES), seq),
                      pl.BlockSpec((None, page, IDX_DIM), pg), pl.BlockSpec((None, page, IDX_DIM), seq)],
            out_specs=pl.BlockSpec((None, nj, 8, LANES), lambda b, j, pt: (b, 0, 0, 0)),
            scratch_shapes=[pltpu.VMEM((nj, 8, LANES), I32)]),
        out_shape=jax.ShapeDtypeStruct((db, nj, 8, LANES), F32),
        compiler_params=_params(2), name="dsa_sample_select")(page_table, qih, wib, pool_ki, kin)

    o = pl.pallas_call(
        functools.partial(_dsa_sample_attend_body, n_pages=n_pages),
        grid_spec=pltpu.PrefetchScalarGridSpec(
            num_scalar_prefetch=1, grid=(db, nj),
            in_specs=[pl.BlockSpec((None, 64, 512), seq),
                      pl.BlockSpec((None, None, 8, LANES), lambda b, j, pt: (b, j, 0, 0)),
                      pl.BlockSpec((None, page, 512), pg), pl.BlockSpec((None, page, 512), pg),
                      pl.BlockSpec((None, page, 512), seq), pl.BlockSpec((None, page, 512), seq)],
            out_specs=pl.BlockSpec((None, 64, 512), seq),
            scratch_shapes=[pltpu.VMEM((64, LANES), F32), pltpu.VMEM((64, LANES), F32),
                            pltpu.VMEM((64, 512), F32)]),
        out_shape=jax.ShapeDtypeStruct((db, 64, 512), F32),
        compiler_params=_params(2), name="dsa_sample_attend")(
            page_table, qbd, bias, pool_k.reshape(n_pool, page, 512), pool_v.reshape(n_pool, page, 512),
            kn, vn)
    o5 = o.reshape(db, N_HEADS_B, 8, N_HEADS_B, HEAD_DIM_B)
    heads = [o5[:, h, :t, h, :] for h in range(N_HEADS_B)]
    return jnp.stack(heads, axis=2).reshape(db, t, 512)


def _pool_out_body(u_ref, pre_ref, battn_ref, x_ref, wpool_ref, scale_ref, wout_ref,
                   x1_ref, hist_ref, ext, *, pos0):
    tt = u_ref.shape[0]
    ti = pl.program_id(1)
    h0 = POOL_HIST

    @pl.when(ti == 0)
    def _():
        ext[0:h0, :] = pre_ref[...]

    ext[h0:h0 + tt, :] = u_ref[...]
    pos = pos0 + ti * tt + lax.broadcasted_iota(I32, (tt, LANES), 0)
    outs = []
    for g, w in enumerate(POOL_WINDOWS):
        cols = slice(g * LANES, (g + 1) * LANES)
        cur = ext[h0:h0 + tt, cols]
        s = cur
        for back in range(1, w):
            s = s + ext[h0 - back:h0 - back + tt, cols]
        cnt = jnp.minimum(pos + 1, w).astype(F32)
        d = (s / cnt - cur).astype(BF16)
        outs.append(jnp.dot(d, wpool_ref[g], preferred_element_type=F32) * scale_ref[:, cols])
    a = jnp.concatenate(outs, axis=1).astype(BF16)
    y = jnp.dot(a, wout_ref[0:512, :], preferred_element_type=F32)
    y = y + jnp.dot(battn_ref[...].astype(BF16), wout_ref[512:1024, :], preferred_element_type=F32)
    x1_ref[...] = x_ref[...] + y
    new_hist = ext[tt:tt + h0, :]
    ext[0:h0, :] = new_hist
    hist_ref[...] = new_hist


def _pool_out(u, prefix, battn, x, w_pool, pool_scale, w_out, pos0):
    bsz, t, _ = u.shape
    d = x.shape[-1]
    tt = 256 if t % 256 == 0 else t
    pre = jnp.pad(prefix, ((0, 0), (POOL_HIST - prefix.shape[1], 0), (0, 0)))
    tile = lambda b, i: (b, i, 0)
    perb = lambda b, i: (b, 0, 0)
    x1, hist = pl.pallas_call(
        functools.partial(_pool_out_body, pos0=pos0),
        grid=(bsz, t // tt),
        in_specs=[pl.BlockSpec((None, tt, 512), tile), pl.BlockSpec((None, POOL_HIST, 512), perb),
                  pl.BlockSpec((None, tt, 512), tile), pl.BlockSpec((None, tt, d), tile),
                  pl.BlockSpec((4, LANES, LANES), lambda b, i: (0, 0, 0)),
                  pl.BlockSpec((1, 512), lambda b, i: (0, 0)),
                  pl.BlockSpec((1024, d), lambda b, i: (0, 0))],
        out_specs=[pl.BlockSpec((None, tt, d), tile), pl.BlockSpec((None, POOL_HIST, 512), perb)],
        out_shape=[jax.ShapeDtypeStruct((bsz, t, d), F32), jax.ShapeDtypeStruct((bsz, POOL_HIST, 512), F32)],
        scratch_shapes=[pltpu.VMEM((POOL_HIST + tt, 512), F32)],
        compiler_params=_params(2, 32), name="pool_outproj")(
            u, pre, battn, x, w_pool.astype(BF16), pool_scale.reshape(1, 512), w_out.astype(BF16))
    return x1, hist[:, 1:, :]


def _conv_body(x_ref, pre_ref, g_ref, win_ref, bin_ref, wdw_ref, bdw_ref, lng_ref, lnb_ref, wout_ref,
               x1_ref, hist_ref, ext, ybuf):
    tt, d = x_ref.shape
    ti = pl.program_id(1)
    h0 = CONV_HIST

    @pl.when(ti == 0)
    def _():
        ext[0:h0, :] = pre_ref[...]

    x = x_ref[...]
    hb = _rms(x, g_ref[...]).astype(BF16)
    a = jnp.dot(hb, win_ref[...], preferred_element_type=F32) + bin_ref[...]
    ext[h0:h0 + tt, :] = a[:, :d] * _sigmoid(a[:, d:])
    rc = min(tt, 128)
    first = h0 - (CONV_WIDTH - 1)
    for r0 in range(0, tt, rc):
        for c0 in range(0, d, LANES):
            cols = slice(c0, c0 + LANES)
            y = jnp.broadcast_to(bdw_ref[:, cols], (rc, LANES))
            for j in range(CONV_WIDTH):
                y = y + ext[first + r0 + j:first + r0 + j + rc, cols] * wdw_ref[j:j + 1, cols]
            ybuf[r0:r0 + rc, cols] = y
    y = ybuf[...]
    mu = jnp.mean(y, axis=-1, keepdims=True)
    yc = y - mu
    var = jnp.mean(yc * yc, axis=-1, keepdims=True)
    yn = yc * lax.rsqrt(var + EPS) * lng_ref[...] + lnb_ref[...]
    act = (yn * _sigmoid(yn)).astype(BF16)
    x1_ref[...] = x + jnp.dot(act, wout_ref[...], preferred_element_type=F32)
    new_hist = ext[tt:tt + h0, :]
    ext[0:h0, :] = new_hist
    hist_ref[...] = new_hist


def _conv_mixer(x, prefix, g, w_in, b_in, w_dw, b_dw, ln_g, ln_b, w_out):
    bsz, t, d = x.shape
    tt = 256 if t % 256 == 0 else t
    pre = jnp.pad(prefix, ((0, 0), (CONV_HIST - prefix.shape[1], 0), (0, 0)))
    wdw = jnp.pad(w_dw, ((0, CONV_HIST - CONV_WIDTH), (0, 0)))
    tile = lambda b, i: (b, i, 0)
    perb = lambda b, i: (b, 0, 0)
    fixed = lambda b, i: (0, 0)
    vec = lambda n: pl.BlockSpec((1, n), fixed)
    x1, hist = pl.pallas_call(
        _conv_body,
        grid=(bsz, t // tt),
        in_specs=[pl.BlockSpec((None, tt, d), tile), pl.BlockSpec((None, CONV_HIST, d), perb), vec(d),
                  pl.BlockSpec((d, 2 * d), fixed), vec(2 * d), pl.BlockSpec((CONV_HIST, d), fixed), vec(d),
                  vec(d), vec(d), pl.BlockSpec((d, d), fixed)],
        out_specs=[pl.BlockSpec((None, tt, d), tile), pl.BlockSpec((None, CONV_HIST, d), perb)],
        out_shape=[jax.ShapeDtypeStruct((bsz, t, d), F32), jax.ShapeDtypeStruct((bsz, CONV_HIST, d), F32)],
        scratch_shapes=[pltpu.VMEM((CONV_HIST + tt, d), F32), pltpu.VMEM((tt, d), F32)],
        compiler_params=_params(2, 40), name="conv_mixer")(
            x, pre, g.reshape(1, d), w_in.astype(BF16), b_in.reshape(1, 2 * d), wdw, b_dw.reshape(1, d),
            ln_g.reshape(1, d), ln_b.reshape(1, d), w_out.astype(BF16))
    return x1, hist[:, CONV_HIST - (CONV_WIDTH - 1):, :]


def _memkv_body(mem_ref, g_ref, wk_ref, wv_ref, k_ref, v_ref):
    mb = _rms(mem_ref[...], g_ref[...]).astype(BF16)
    k_ref[...] = jnp.dot(mb, wk_ref[...], preferred_element_type=F32)
    v_ref[...] = jnp.dot(mb, wv_ref[...], preferred_element_type=F32)


def _mem_kv(mem, g_mem, w_ck, w_cv):
    bsz, m, d = mem.shape
    depth = g_mem.shape[0]
    out = jax.ShapeDtypeStruct((depth, bsz, m, d), F32)
    wspec = pl.BlockSpec((None, d, d), lambda l, b: (l, 0, 0))
    ospec = pl.BlockSpec((None, None, m, d), lambda l, b: (l, b, 0, 0))
    return pl.pallas_call(
        _memkv_body, grid=(depth, bsz),
        in_specs=[pl.BlockSpec((None, m, d), lambda l, b: (b, 0, 0)),
                  pl.BlockSpec((None, 1, d), lambda l, b: (l, 0, 0)), wspec, wspec],
        out_specs=[ospec, ospec], out_shape=[out, out],
        compiler_params=_params(2, 32), name="mem_kv")(
            mem, g_mem.reshape(depth, 1, d), w_ck.astype(BF16), w_cv.astype(BF16))


def _cross_body(x_ref, g_ref, wq_ref, mk_ref, mv_ref, wo_ref, o_ref):
    x = x_ref[...]
    d = x.shape[-1]
    hd = d // N_HEADS_X
    hb = _rms(x, g_ref[...]).astype(BF16)
    q = (jnp.dot(hb, wq_ref[...], preferred_element_type=F32) * (hd ** -0.5)).astype(BF16)
    outs = []
    for h in range(N_HEADS_X):
        cols = slice(h * hd, (h + 1) * hd)
        s = lax.dot_general(q[:, cols], mk_ref[:, cols].astype(BF16), _NT, preferred_element_type=F32)
        p = jnp.exp(s - jnp.max(s, axis=-1, keepdims=True))
        l = jnp.sum(p, axis=-1, keepdims=True)
        o = jnp.dot(p.astype(BF16), mv_ref[:, cols].astype(BF16), preferred_element_type=F32)
        outs.append(o / l)
    o = jnp.concatenate(outs, axis=1).astype(BF16)
    o_ref[...] = x + jnp.dot(o, wo_ref[...], preferred_element_type=F32)


def _cross_attn(x, g, w_cq, mem_k, mem_v, w_co):
    bsz, t, d = x.shape
    m = mem_k.shape[1]
    tt = 256 if t % 256 == 0 else t
    tile = lambda b, i: (b, i, 0)
    perb = lambda b, i: (b, 0, 0)
    fixed = lambda b, i: (0, 0)
    return pl.pallas_call(
        _cross_body, grid=(bsz, t // tt),
        in_specs=[pl.BlockSpec((None, tt, d), tile), pl.BlockSpec((1, d), fixed), pl.BlockSpec((d, d), fixed),
                  pl.BlockSpec((None, m, d), perb), pl.BlockSpec((None, m, d), perb),
                  pl.BlockSpec((d, d), fixed)],
        out_specs=pl.BlockSpec((None, tt, d), tile),
        out_shape=jax.ShapeDtypeStruct((bsz, t, d), F32),
        compiler_params=_params(2, 32), name="cross_attn")(
            x, g.reshape(1, d), w_cq.astype(BF16), mem_k, mem_v, w_co.astype(BF16))


def _store_row_tiles(ref, x):
    for s in range(ref.shape[1]):
        ref[:, s, :] = x[:, s * LANES:(s + 1) * LANES]


def _load_row_tiles(ref):
    return jnp.concatenate([ref[:, s, :] for s in range(ref.shape[1])], axis=1)


def _router_body(x_ref, g_ref, wr_ref, br_ref, h_ref, eid_ref, gate_ref, rank_ref, cnt_ref, run):
    tm = x_ref.shape[0]
    step = pl.program_id(0)

    @pl.when(step == 0)
    def _():
        run[...] = jnp.zeros(run.shape, F32)

    h = _rms(x_ref[...], g_ref[...])
    _store_row_tiles(h_ref, h)
    lg = jnp.dot(h, wr_ref[...], preferred_element_type=F32, precision=lax.Precision.HIGHEST) + br_ref[...]
    lane = lax.broadcasted_iota(I32, (tm, LANES), 1).astype(F32)
    neg = -jnp.inf

    def first_argmax(v):
        mx = jnp.max(v, axis=1, keepdims=True)
        return mx, jnp.min(jnp.where(v == mx, lane, 1e9), axis=1, keepdims=True)

    is_g = lane < N_GROUPS
    gmax, g_sel = first_argmax(jnp.where(is_g, lg, neg))
    g_prob = 1.0 / jnp.sum(jnp.where(is_g, jnp.exp(lg - gmax), 0.0), axis=1, keepdims=True)
    lo = N_GROUPS + g_sel * EXPERTS_PER_GROUP
    el = jnp.where((lane >= lo) & (lane < lo + EXPERTS_PER_GROUP), lg, neg)
    v1, i1 = first_argmax(el)
    v2, i2 = first_argmax(jnp.where(lane == i1, neg, el))
    e2 = jnp.exp(v2 - v1)
    gate1 = g_prob / (1.0 + e2)
    gate2 = g_prob * e2 / (1.0 + e2)
    id1 = i1 - N_GROUPS
    id2 = i2 - N_GROUPS
    oh1 = jnp.where(lane == id1, 1.0, 0.0)
    oh2 = jnp.where(lane == id2, 1.0, 0.0)
    r = lax.broadcasted_iota(I32, (tm, tm), 0)
    c = lax.broadcasted_iota(I32, (tm, tm), 1)
    strict = jnp.where(c < r, 1.0, 0.0).astype(BF16)
    before = jnp.dot(strict, (oh1 + oh2).astype(BF16), preferred_element_type=F32) + run[...]
    rank1 = jnp.sum(oh1 * before, axis=1, keepdims=True)
    rank2 = jnp.sum(oh2 * before, axis=1, keepdims=True)
    run[...] = run[...] + jnp.sum(oh1 + oh2, axis=0, keepdims=True)
    two = lax.broadcasted_iota(I32, (tm, 2), 1)
    eid_ref[...] = jnp.where(two == 0, id1, id2).astype(I32)
    gate_ref[...] = jnp.where(two == 0, gate1, gate2)
    rank_ref[...] = jnp.where(two == 0, rank1, rank2).astype(I32)
    cnt_ref[...] = run[...].astype(I32)


def _router(x2d, g, w_rg, b_rg, w_re, b_re):
    n, d = x2d.shape
    tm = _token_tile(n)
    w_r = jnp.concatenate([w_rg, w_re.transpose(1, 0, 2).reshape(d, N_EXPERTS)], axis=1)
    w_r = jnp.pad(w_r, ((0, 0), (0, LANES - w_r.shape[1])))
    b_r = jnp.pad(jnp.concatenate([b_rg, b_re.reshape(N_EXPERTS)]), (0, LANES - N_GROUPS - N_EXPERTS))
    row = lambda i: (i, 0)
    fixed = lambda i: (0, 0)
    pair = pl.BlockSpec((tm, 2), row)
    return pl.pallas_call(
        _router_body, grid=(n // tm,),
        in_specs=[pl.BlockSpec((tm, d), row), pl.BlockSpec((1, d), fixed), pl.BlockSpec((d, LANES), fixed),
                  pl.BlockSpec((1, LANES), fixed)],
        out_specs=[pl.BlockSpec((tm, d // LANES, LANES), lambda i: (i, 0, 0)), pair, pair, pair,
                   pl.BlockSpec((1, LANES), fixed)],
        out_shape=[jax.ShapeDtypeStruct((n, d // LANES, LANES), F32), jax.ShapeDtypeStruct((n, 2), I32),
                   jax.ShapeDtypeStruct((n, 2), F32), jax.ShapeDtypeStruct((n, 2), I32),
                   jax.ShapeDtypeStruct((1, LANES), I32)],
        scratch_shapes=[pltpu.VMEM((1, LANES), F32)],
        compiler_params=_params(1, 32), name="moe_router")(x2d, g.reshape(1, d), w_r, b_r.reshape(1, LANES))


def _dispatch_body(slot_ref, pend_ref, h_hbm, xs_hbm, zeros, sem, *, tm, tb, nb):
    step = pl.program_id(0)

    @pl.when(step == 0)
    def _():
        zeros[...] = jnp.zeros(zeros.shape, F32)
        n_used = pend_ref[N_EXPERTS - 1] // tb

        def zero_copy(blk):
            return pltpu.make_async_copy(zeros, xs_hbm.at[pl.ds(pl.multiple_of(blk * tb, tb), tb)], sem.at[0])

        def seg_last_block(e):
            start = jnp.where(e == 0, 0, pend_ref[jnp.maximum(e - 1, 0)])
            return pend_ref[e] > start, jnp.maximum(pend_ref[e] // tb - 1, 0)

        def each(fn):
            def seg(e, carry):
                nonempty, blk = seg_last_block(e)

                @pl.when(nonempty)
                def _():
                    fn(zero_copy(blk))
                return carry
            lax.fori_loop(0, N_EXPERTS, seg, 0)

            def tail(blk, carry):
                @pl.when(blk >= n_used)
                def _():
                    fn(zero_copy(blk))
                return carry
            lax.fori_loop(0, nb, tail, 0)

        each(lambda cp: cp.start())
        each(lambda cp: cp.wait())

    base = step * tm

    def row_copy(r, j):
        return pltpu.make_async_copy(h_hbm.at[base + r], xs_hbm.at[slot_ref[2 * (base + r) + j]], sem.at[1])

    def rstart(r, carry):
        row_copy(r, 0).start()
        row_copy(r, 1).start()
        return carry
    lax.fori_loop(0, tm, rstart, 0)

    def rwait(r, carry):
        row_copy(r, 0).wait()
        row_copy(r, 1).wait()
        return carry
    lax.fori_loop(0, tm, rwait, 0)


def _dispatch(h2, slot_flat, pend, n_rows, tb):
    n, s, l = h2.shape
    tm = _token_tile(n)
    return pl.pallas_call(
        functools.partial(_dispatch_body, tm=tm, tb=tb, nb=n_rows // tb),
        grid_spec=pltpu.PrefetchScalarGridSpec(
            num_scalar_prefetch=2, grid=(n // tm,),
            in_specs=[pl.BlockSpec(memory_space=pl.ANY)],
            out_specs=pl.BlockSpec(memory_space=pl.ANY),
            scratch_shapes=[pltpu.VMEM((tb, s, l), F32), pltpu.SemaphoreType.DMA((2,))]),
        out_shape=jax.ShapeDtypeStruct((n_rows, s, l), F32),
        compiler_params=_params(1), name="moe_dispatch")(slot_flat, pend, h2)


def _expert_body(bexp_ref, nvalid_ref, xs_ref, w1_ref, w3_ref, w2_ref, ys_ref):
    del bexp_ref
    i = pl.program_id(0)

    @pl.when(i < nvalid_ref[0])
    def _():
        xb = _load_row_tiles(xs_ref).astype(BF16)
        a = jnp.dot(xb, w1_ref[...], preferred_element_type=F32)
        b = jnp.dot(xb, w3_ref[...], preferred_element_type=F32)
        hmid = (a * _sigmoid(a) * b).astype(BF16)
        _store_row_tiles(ys_ref, jnp.dot(hmid, w2_ref[...], preferred_element_type=F32))

    @pl.when(i >= nvalid_ref[0])
    def _():
        ys_ref[...] = jnp.zeros(ys_ref.shape, F32)


def _experts(xs, blk_exp, nvalid, w1, w3, w2, tb):
    p, s, l = xs.shape
    d = s * l
    ff = w1.shape[-1]
    nb = p // tb
    rows_in = lambda i, be, nv: (jnp.minimum(i, jnp.maximum(nv[0] - 1, 0)), 0, 0)
    wmap = lambda i, be, nv: (be[i], 0, 0)
    return pl.pallas_call(
        _expert_body,
        grid_spec=pltpu.PrefetchScalarGridSpec(
            num_scalar_prefetch=2, grid=(nb,),
            in_specs=[pl.BlockSpec((tb, s, l), rows_in), pl.BlockSpec((None, d, ff), wmap),
                      pl.BlockSpec((None, d, ff), wmap), pl.BlockSpec((None, ff, d), wmap)],
            out_specs=pl.BlockSpec((tb, s, l), lambda i, be, nv: (i, 0, 0))),
        out_shape=jax.ShapeDtypeStruct((p, s, l), F32),
        compiler_params=_params(1, 32), name="moe_experts")(blk_exp, nvalid, xs, w1, w3, w2)


def _combine_body(slot_ref, x_ref, gate_ref, gf_ref, ys_hbm, o_ref, buf, sem, *, final_norm):
    tm = x_ref.shape[0]
    base = pl.program_id(0) * tm * 2

    def row_copy(r, j):
        return pltpu.make_async_copy(ys_hbm.at[slot_ref[base + 2 * r + j]], buf.at[j, r], sem.at[0])

    def start(r, carry):
        row_copy(r, 0).start()
        row_copy(r, 1).start()
        return carry
    lax.fori_loop(0, tm, start, 0)

    def wait(r, carry):
        row_copy(r, 0).wait()
        row_copy(r, 1).wait()
        return carry
    lax.fori_loop(0, tm, wait, 0)

    gate = gate_ref[...]
    y = x_ref[...] + _load_row_tiles(buf.at[0]) * gate[:, 0:1] + _load_row_tiles(buf.at[1]) * gate[:, 1:2]
    if final_norm:
        y = _rms(y, gf_ref[...])
    o_ref[...] = y


def _combine(x2d, gates, slot_flat, ys, g_final):
    n, d = x2d.shape
    tm = 256 if n % 256 == 0 else n
    final_norm = g_final is not None
    gf = (g_final if final_norm else jnp.ones((d,), F32)).reshape(1, d)
    row = lambda i, s: (i, 0)
    return pl.pallas_call(
        functools.partial(_combine_body, final_norm=final_norm),
        grid_spec=pltpu.PrefetchScalarGridSpec(
            num_scalar_prefetch=1, grid=(n // tm,),
            in_specs=[pl.BlockSpec((tm, d), row), pl.BlockSpec((tm, 2), row),
                      pl.BlockSpec((1, d), lambda i, s: (0, 0)), pl.BlockSpec(memory_space=pl.ANY)],
            out_specs=pl.BlockSpec((tm, d), row),
            scratch_shapes=[pltpu.VMEM((2, tm, d // LANES, LANES), F32), pltpu.SemaphoreType.DMA((1,))]),
        out_shape=jax.ShapeDtypeStruct((n, d), F32),
        compiler_params=_params(1, 32), name="moe_combine")(slot_flat, x2d, gates, gf, ys)


def _moe(x2d, g, w_rg, b_rg, w_re, b_re, w1, w3, w2, g_final):
    n, _ = x2d.shape
    tb = 256 if n >= 2048 else 128
    h2, eid, gates, rank, counts = _router(x2d, g, w_rg, b_rg, w_re, b_re)
    counts = counts[0, :N_EXPERTS]
    pcounts = (counts + tb - 1) // tb * tb
    pend = jnp.cumsum(pcounts)
    pstart = pend - pcounts
    slot_flat = (pstart[eid] + rank).reshape(2 * n).astype(I32)
    nb = -(-(2 * n + N_EXPERTS * (tb - 1)) // tb)
    blk_exp = jnp.minimum(jnp.searchsorted(pend, jnp.arange(nb, dtype=I32) * tb, side="right"),
                          N_EXPERTS - 1).astype(I32)
    nvalid = (pend[-1:] // tb).astype(I32)
    xs = _dispatch(h2, slot_flat, pend.astype(I32), nb * tb, tb)
    ys = _experts(xs, blk_exp, nvalid, w1, w3, w2, tb)
    return _combine(x2d, gates, slot_flat, ys, g_final)


def _trunk(x, pos0, pool_prefix, conv_prefix, mem_k, mem_v, attn_fn, p, we):
    bsz, t, d = x.shape
    n = bsz * t
    proj = _inproj(x.reshape(n, d), p["g_mix"][0], p["w_in_even"][0], with_vt=attn_fn == "prompt")
    u, qb, k, v, kb, qib, ki, kib, wi = proj[:9]
    r3 = lambda a: a.reshape(bsz, t, a.shape[-1])
    if attn_fn == "prompt":
        vt = proj[9].reshape(bsz, t // KEY_CHUNK, 512, KEY_CHUNK)
        battn = _dsa_prompt(r3(qib), r3(wi), r3(kib), r3(qb), r3(kb), vt)
    else:
        battn = attn_fn(r3(qb).astype(F32), r3(k), r3(v), r3(qib).astype(F32), r3(wi), r3(ki))
    x, new_pool = _pool_out(r3(u), pool_prefix[0], battn, x, p["w_pool"][0], p["pool_scale"][0],
                            p["w_out_even"][0], pos0)
    x = _cross_attn(x, p["g_cross"][0], p["w_cq"][0], mem_k[0], mem_v[0], p["w_co"][0])
    x = _moe(x.reshape(n, d), p["g_ffn"][0], p["w_rg"][0], p["b_rg"][0], p["w_re"][0], p["b_re"][0],
             we[0][0], we[1][0], we[2][0], None).reshape(bsz, t, d)
    x, new_conv = _conv_mixer(x, conv_prefix[0], p["g_mix"][1], p["w_conv_in"][0], p["b_conv_in"][0],
                              p["w_dw"][0], p["b_dw"][0], p["ln_g"][0], p["ln_b"][0], p["w_conv_out"][0])
    x = _cross_attn(x, p["g_cross"][1], p["w_cq"][1], mem_k[1], mem_v[1], p["w_co"][1])
    y = _moe(x.reshape(n, d), p["g_ffn"][1], p["w_rg"][1], p["b_rg"][1], p["w_re"][1], p["b_re"][1],
             we[0][1], we[1][1], we[2][1], p["g_final"]).reshape(bsz, t, d)
    hk = lambda a: a.reshape(1, bsz, t, N_HEADS_B, HEAD_DIM_B)
    return y, hk(k), hk(v), ki.reshape(1, bsz, t, IDX_DIM), new_pool[None], new_conv[None]


def kernel(x_prompt, x_sample, mem_prompt, cache_attn_k, cache_attn_v, cache_idx_k, page_table,
           state_pool, state_conv, cache_mem_k, cache_mem_v, g_mix, g_cross, g_mem, g_ffn, g_final,
           w_in_even, w_pool, pool_scale, w_out_even, w_conv_in, b_conv_in, w_dw, b_dw, ln_g, ln_b,
           w_conv_out, w_cq, w_ck, w_cv, w_co, w_rg, b_rg, w_re, b_re, w_e1, w_e3, w_e2):
    p = {"g_mix": g_mix, "g_cross": g_cross, "g_ffn": g_ffn, "g_final": g_final,
         "w_in_even": w_in_even, "w_pool": w_pool, "pool_scale": pool_scale, "w_out_even": w_out_even,
         "w_conv_in": w_conv_in, "b_conv_in": b_conv_in, "w_dw": w_dw, "b_dw": b_dw, "ln_g": ln_g,
         "ln_b": ln_b, "w_conv_out": w_conv_out, "w_cq": w_cq, "w_co": w_co, "w_rg": w_rg, "b_rg": b_rg,
         "w_re": w_re, "b_re": b_re}
    we = (w_e1.astype(BF16), w_e3.astype(BF16), w_e2.astype(BF16))
    bsz, t, d = x_prompt.shape
    db = x_sample.shape[0]
    depth = g_mem.shape[0]
    n_mem = mem_prompt.shape[1]

    mem_k_p, mem_v_p = _mem_kv(mem_prompt, g_mem, w_ck, w_cv)
    zeros_pool = jnp.zeros((1, bsz, POOL_HIST - 1, 512), F32)
    zeros_conv = jnp.zeros((1, bsz, CONV_WIDTH - 1, d), F32)
    y_p, nk_p, nv_p, nki_p, npool_p, nconv_p = _trunk(
        x_prompt, 0, zeros_pool, zeros_conv, mem_k_p, mem_v_p, "prompt", p, we)

    def sample_attn(q, k, v, qi, wi, ki):
        return _dsa_sample(q, k, v, qi, wi, ki, cache_attn_k[0], cache_attn_v[0], cache_idx_k[0], page_table)

    past = page_table.shape[1] * cache_idx_k.shape[2]
    y_s, nk_s, nv_s, nki_s, npool_s, nconv_s = _trunk(
        x_sample, past, state_pool, state_conv, cache_mem_k.reshape(depth, db, n_mem, d),
        cache_mem_v.reshape(depth, db, n_mem, d), sample_attn, p, we)

    mem_shape = (depth, bsz, n_mem, N_HEADS_X, d // N_HEADS_X)
    return (y_p, y_s, nk_p, nv_p, nki_p, npool_p, nconv_p, mem_k_p.reshape(mem_shape),
            mem_v_p.reshape(mem_shape), nk_s, nv_s, nki_s, npool_s, nconv_s)
```

```python
import functools
import math

import jax
import jax.numpy as jnp
from jax import lax
from jax.experimental import pallas as pl
from jax.experimental.pallas import tpu as pltpu

F32 = jnp.float32
BF16 = jnp.bfloat16
I32 = jnp.int32

EPS = 1e-6
POOL_WINDOWS = (2, 4, 8, 16)
POOL_HIST = 16
CONV_WIDTH = 31
CONV_HIST = 32
N_HEADS_B = 8
HEAD_DIM_B = 64
N_IDX_HEADS = 8
IDX_DIM = 64
TOPK_MAX = 256
Q_BLOCK = 128
KEY_CHUNK = 256
N_HEADS_X = 4
N_GROUPS = 4
EXPERTS_PER_GROUP = 8
N_EXPERTS = N_GROUPS * EXPERTS_PER_GROUP
LANES = 128
INT_MIN = -(2 ** 31)
NEG_BIG = -1e30
MOST_NEGATIVE = -3.4028234663852886e38
MIB = 1024 * 1024

_NT = (((1,), (1,)), ((), ()))


def _params(n_axes, vmem_mib=None, **kw):
    if vmem_mib is not None:
        kw["vmem_limit_bytes"] = vmem_mib * MIB
    return pltpu.CompilerParams(dimension_semantics=("arbitrary",) * n_axes, **kw)


def _rms(x, g):
    return x * lax.rsqrt(jnp.mean(x * x, axis=-1, keepdims=True) + EPS) * g


def _sigmoid(x):
    return 1.0 / (1.0 + jnp.exp(-x))


def _float_of_rank(u):
    key = u ^ INT_MIN
    return pltpu.bitcast(jnp.where(key >= 0, key, key ^ 0x7FFFFFFF), F32)


def _topk_threshold(count_ge, shape, topk):
    def bit_step(it, carry):
        prefix, n_ge = carry
        cand = prefix | (jnp.int32(1) << (31 - it))
        cnt = count_ge(_float_of_rank(cand))
        ok = cnt >= topk
        return jnp.where(ok, cand, prefix), jnp.where(ok, cnt, n_ge)

    prefix, n_ge = lax.fori_loop(0, 32, bit_step, (jnp.zeros(shape, I32), jnp.full(shape, 2.0 ** 30, F32)))
    lowest_finite_rank = INT_MIN + 0x00800000
    has_k = (prefix ^ INT_MIN) >= lowest_finite_rank
    return _float_of_rank(prefix), has_k, n_ge


def _token_tile(n):
    return 512 if n % 512 == 0 else n


def _inproj_body(x_ref, g_ref, w_ref, *refs, with_vt):
    if with_vt:
        wvt_ref, refs = refs[0], refs[1:]
    u_ref, q_ref, k_ref, v_ref, kb_ref, qi_ref, ki_ref, kib_ref, wi_ref = refs[:9]
    hb = _rms(x_ref[...], g_ref[...]).astype(BF16)

    def proj(c0, c1):
        return jnp.dot(hb, w_ref[:, c0:c1], preferred_element_type=F32)

    u_ref[...] = proj(0, 512)
    q_ref[...] = proj(512, 1024).astype(BF16)
    k = proj(1024, 1536)
    v = proj(1536, 2048)
    for h in range(N_HEADS_B):
        k_ref[:, h, :] = k[:, h * HEAD_DIM_B:(h + 1) * HEAD_DIM_B]
        v_ref[:, h, :] = v[:, h * HEAD_DIM_B:(h + 1) * HEAD_DIM_B]
    kb_ref[...] = k.astype(BF16)
    qi_ref[...] = proj(2048, 2560).astype(BF16)
    ki = proj(2560, 2688)[:, :IDX_DIM]
    ki_ref[...] = ki
    kib_ref[...] = ki.astype(BF16)
    wi_ref[...] = proj(2688, 2816)[:, :N_IDX_HEADS] * (N_IDX_HEADS ** -0.5)
    if with_vt:
        vt_ref = refs[9]
        vt = lax.dot_general(wvt_ref[...], hb, _NT, preferred_element_type=F32)
        for c in range(vt_ref.shape[0]):
            vt_ref[c] = vt[:, c * KEY_CHUNK:(c + 1) * KEY_CHUNK].astype(BF16)


def _pack_w_in(w_in):
    d = w_in.shape[0]
    main = w_in[:, :2560]
    ki = jnp.pad(w_in[:, 2560:2624], ((0, 0), (0, 64)))
    wi = jnp.pad(w_in[:, 2624:2632], ((0, 0), (0, 120)))
    del d
    return jnp.concatenate([main, ki, wi], axis=1).astype(BF16)


def _inproj(x2d, g, w_in, with_vt):
    n, d = x2d.shape
    tm = _token_tile(n)
    w = _pack_w_in(w_in)
    row = lambda i: (i, 0)
    fixed = lambda i: (0, 0)
    in_specs = [pl.BlockSpec((tm, d), row), pl.BlockSpec((1, d), fixed), pl.BlockSpec(w.shape, fixed)]
    args = [x2d, g.reshape(1, d), w]
    if with_vt:
        wvt = w_in[:, 1536:2048].T.astype(BF16)
        in_specs.append(pl.BlockSpec(wvt.shape, fixed))
        args.append(wvt)
    heads = (n, N_HEADS_B, HEAD_DIM_B)
    shapes = [((n, 512), F32), ((n, 512), BF16), (heads, F32), (heads, F32), ((n, 512), BF16),
              ((n, 512), BF16), ((n, IDX_DIM), F32), ((n, IDX_DIM), BF16), ((n, N_IDX_HEADS), F32)]
    out_shape = [jax.ShapeDtypeStruct(s, t) for s, t in shapes]
    out_specs = [pl.BlockSpec((tm,) + s[1:], (lambda i: (i, 0, 0)) if len(s) == 3 else row) for s, _ in shapes]
    if with_vt:
        cpt = tm // KEY_CHUNK
        out_shape.append(jax.ShapeDtypeStruct((n // KEY_CHUNK, 512, KEY_CHUNK), BF16))
        out_specs.append(pl.BlockSpec((cpt, 512, KEY_CHUNK), lambda i: (i, 0, 0)))
    return pl.pallas_call(
        functools.partial(_inproj_body, with_vt=with_vt),
        grid=(n // tm,), in_specs=in_specs, out_specs=out_specs, out_shape=out_shape,
        compiler_params=_params(1, 48), name="inproj_even")(*args)


def _dsa_prompt_body(qi_ref, wit_ref, ki_ref, q_ref, kb_ref, vt_ref, o_ref, score_s, bias, s_all, p_all, *accs,
                     topk):
    ck, qb = KEY_CHUNK, Q_BLOCK
    i = pl.program_id(1)
    n_keys = (i + 1) * qb
    nch = (n_keys + ck - 1) // ck

    qis = [qi_ref[:, h * IDX_DIM:(h + 1) * IDX_DIM] for h in range(N_IDX_HEADS)]
    wit = wit_ref[...]
    q_pos = i * qb + lax.broadcasted_iota(I32, (ck, qb), 1)
    k_row = lax.broadcasted_iota(I32, (ck, qb), 0)

    def score_chunk(c, carry):
        off = pl.multiple_of(c * ck, ck)
        kc = ki_ref[pl.ds(off, ck), :]
        for h in range(N_IDX_HEADS):
            s_all[h] = lax.dot_general(kc, qis[h], _NT, preferred_element_type=F32)
        sc = jnp.zeros((ck, qb), F32)
        for h in range(N_IDX_HEADS):
            sc = sc + jnp.maximum(s_all[h], 0.0) * wit[h:h + 1, :]
        score_s[pl.ds(off, ck), :] = jnp.where(k_row + off <= q_pos, sc, -jnp.inf)
        return carry

    lax.fori_loop(0, nch, score_chunk, 0)

    def count(pred):
        def body(c, cnt):
            off = pl.multiple_of(c * ck, ck)
            ind = jnp.where(pred(score_s[pl.ds(off, ck), :]), 1.0, 0.0)
            return cnt + jnp.sum(ind.reshape(8, ck // 8, qb), axis=0)
        cnt = lax.fori_loop(0, nch, body, jnp.zeros((ck // 8, qb), F32))
        return jnp.sum(cnt, axis=0, keepdims=True)

    thr, has_k, n_ge = _topk_threshold(lambda t: count(lambda s: s >= t), (1, qb), topk)
    tie_excess = jnp.max(jnp.where(has_k & (n_ge > topk), 1.0, 0.0)) > 0.5

    @pl.when(jnp.logical_not(tie_excess))
    def _():
        tsel = jnp.where(has_k, thr, MOST_NEGATIVE)

        def body(c, carry):
            off = pl.multiple_of(c * ck, ck)
            bias[pl.ds(off, ck), :] = jnp.where(score_s[pl.ds(off, ck), :] >= tsel, 0.0, NEG_BIG)
            return carry
        lax.fori_loop(0, nch, body, 0)

    @pl.when(tie_excess)
    def _():
        need = topk - count(lambda s: s > thr)
        thr_k = jnp.where(has_k, thr, -jnp.inf)
        r = lax.broadcasted_iota(I32, (ck, ck), 0)
        cidx = lax.broadcasted_iota(I32, (ck, ck), 1)
        tri = jnp.where(cidx <= r, 1.0, 0.0).astype(BF16)

        def body(c, run):
            off = pl.multiple_of(c * ck, ck)
            sc = score_s[pl.ds(off, ck), :]
            eq = jnp.where(sc == thr, 1.0, 0.0)
            incl = jnp.dot(tri, eq.astype(BF16), preferred_element_type=F32)
            rank = incl - eq + run
            sel = (sc > thr_k) | ((eq > 0.0) & (rank < need))
            bias[pl.ds(off, ck), :] = jnp.where(sel, 0.0, NEG_BIG)
            return run + jnp.sum(eq, axis=0, keepdims=True)
        lax.fori_loop(0, nch, body, jnp.zeros((1, qb), F32))

    qf = q_ref[...].astype(F32)
    lane = lax.broadcasted_iota(I32, (qb, LANES), 1)
    qp = []
    for h in range(N_HEADS_B):
        pair = qf[:, (h // 2) * LANES:(h // 2 + 1) * LANES]
        keep = (lane >= HEAD_DIM_B) if h % 2 else (lane < HEAD_DIM_B)
        qp.append((jnp.where(keep, pair, 0.0) * (HEAD_DIM_B ** -0.5)).astype(BF16))
    for a in accs:
        a[...] = jnp.zeros(a.shape, F32)

    def fold(x, op):
        y = x.reshape(8, ck // 8, qb)
        r = op(op(op(y[0], y[1]), op(y[2], y[3])), op(op(y[4], y[5]), op(y[6], y[7])))
        return jnp.max(r, axis=0, keepdims=True) if op is jnp.maximum else jnp.sum(r, axis=0, keepdims=True)

    def attend_chunk(c, carry):
        off = pl.multiple_of(c * ck, ck)
        b = bias[pl.ds(off, ck), :]
        ms, ls = carry
        new_m, new_l, alphas = [], [], []
        for h in range(N_HEADS_B):
            k2 = kb_ref[pl.ds(off, ck), (h // 2) * LANES:(h // 2 + 1) * LANES]
            s_all[h] = lax.dot_general(k2, qp[h], _NT, preferred_element_type=F32) + b
        for h in range(N_HEADS_B):
            s = s_all[h]
            m_new = jnp.maximum(ms[h], fold(s, jnp.maximum))
            alphas.append(jnp.exp(ms[h] - m_new))
            p = jnp.exp(s - m_new)
            new_l.append(alphas[h] * ls[h] + fold(p, jnp.add))
            new_m.append(m_new)
            p_all[h] = p.astype(BF16)
        for h in range(N_HEADS_B):
            rows = slice(h * HEAD_DIM_B, (h + 1) * HEAD_DIM_B)
            pv = jnp.dot(vt_ref[c, rows, :], p_all[h], preferred_element_type=F32)
            accs[h][...] = accs[h][...] * alphas[h] + pv
        return tuple(new_m), tuple(new_l)

    init = (tuple(jnp.full((1, qb), NEG_BIG, F32) for _ in range(N_HEADS_B)),
            tuple(jnp.zeros((1, qb), F32) for _ in range(N_HEADS_B)))
    _, ls = lax.fori_loop(0, nch, attend_chunk, init)
    o = jnp.concatenate([accs[h][...] / ls[h] for h in range(N_HEADS_B)], axis=0)
    o_ref[...] = o.T.astype(BF16)


def _dsa_prompt(qi, wi, kib, q, kb, vt):
    bsz, t, _ = q.shape
    nblk = t // Q_BLOCK
    topk = min(TOPK_MAX, t // 4)
    wit = wi.reshape(bsz, nblk, Q_BLOCK, N_IDX_HEADS).swapaxes(2, 3)
    blk = lambda b, i: (b, i, 0)
    whole = lambda b, i: (b, 0, 0)
    once = pl.Buffered(1)
    return pl.pallas_call(
        functools.partial(_dsa_prompt_body, topk=topk),
        grid=(bsz, nblk),
        in_specs=[
            pl.BlockSpec((None, Q_BLOCK, 512), blk),
            pl.BlockSpec((None, None, N_IDX_HEADS, Q_BLOCK), lambda b, i: (b, i, 0, 0)),
            pl.BlockSpec((None, t, IDX_DIM), whole, pipeline_mode=once),
            pl.BlockSpec((None, Q_BLOCK, 512), blk),
            pl.BlockSpec((None, t, 512), whole, pipeline_mode=once),
            pl.BlockSpec((None, t // KEY_CHUNK, 512, KEY_CHUNK), lambda b, i: (b, 0, 0, 0),
                         pipeline_mode=once),
        ],
        out_specs=pl.BlockSpec((None, Q_BLOCK, 512), blk),
        out_shape=jax.ShapeDtypeStruct((bsz, t, 512), BF16),
        scratch_shapes=[pltpu.VMEM((t, Q_BLOCK), F32), pltpu.VMEM((t, Q_BLOCK), F32),
                        pltpu.VMEM((N_HEADS_B, KEY_CHUNK, Q_BLOCK), F32),
                        pltpu.VMEM((N_HEADS_B, KEY_CHUNK, Q_BLOCK), BF16)]
        + [pltpu.VMEM((HEAD_DIM_B, Q_BLOCK), F32)] * N_HEADS_B,
        compiler_params=_params(2, 48), name="dsa_prompt")(qi, wit, kib, q, kb, vt)


SAMPLE_GROUP_PAGES = 8


def _select_lane_keys(skey, bias_ref, nj, topk):
    def count(pred):
        def body(jj, cnt):
            return cnt + jnp.where(pred(skey[jj]), 1.0, 0.0)
        cnt = lax.fori_loop(0, nj, body, jnp.zeros((8, LANES), F32))
        return jnp.sum(cnt, axis=1, keepdims=True)

    thr, has_k, n_ge = _topk_threshold(lambda t: count(lambda s: s >= t), (8, 1), topk)
    tie_excess = jnp.max(jnp.where(has_k & (n_ge > topk), 1.0, 0.0)) > 0.5

    @pl.when(jnp.logical_not(tie_excess))
    def _():
        tsel = jnp.where(has_k, thr, MOST_NEGATIVE)

        def body(jj, carry):
            bias_ref[jj] = jnp.where(skey[jj] >= tsel, 0.0, NEG_BIG)
            return carry
        lax.fori_loop(0, nj, body, 0)

    @pl.when(tie_excess)
    def _():
        need = topk - count(lambda s: s > thr)
        thr_k = jnp.where(has_k, thr, -jnp.inf)
        a = lax.broadcasted_iota(I32, (LANES, LANES), 0)
        bcol = lax.broadcasted_iota(I32, (LANES, LANES), 1)
        upper = jnp.where(a <= bcol, 1.0, 0.0).astype(BF16)

        def body(jj, run):
            sc = skey[jj]
            eq = jnp.where(sc == thr, 1.0, 0.0)
            incl = jnp.dot(eq.astype(BF16), upper, preferred_element_type=F32)
            rank = incl - eq + run
            sel = (sc > thr_k) | ((eq > 0.0) & (rank < need))
            bias_ref[jj] = jnp.where(sel, 0.0, NEG_BIG)
            return run + jnp.sum(eq, axis=1, keepdims=True)
        lax.fori_loop(0, nj, body, jnp.zeros((8, 1), F32))


def _dsa_sample_body(pt_ref, qih_ref, wib_ref, qh_ref, kinew_ref, knew_ref, vnew_ref,
                     kipool, kpool, vpool, o_ref, kibuf, kbuf, vbuf, score_s, bias, sem,
                     *, topk, n_pages, t_new, page):
    b = pl.program_id(0)
    grp = SAMPLE_GROUP_PAGES
    n_groups = n_pages // grp
    gk = grp * page

    def ki_copy(j):
        return pltpu.make_async_copy(kipool.at[pt_ref[b, j]], kibuf.at[pl.ds(j * page, page)], sem.at[0])

    def group_copies(g, slot):
        cps = []
        for jj in range(grp):
            pg = pt_ref[b, g * grp + jj]
            dst = pl.ds(jj * page, page)
            cps.append(pltpu.make_async_copy(kpool.at[pg], kbuf.at[slot, dst], sem.at[1 + slot]))
            cps.append(pltpu.make_async_copy(vpool.at[pg], vbuf.at[slot, dst], sem.at[1 + slot]))
        return cps

    def ki_start(j, carry):
        ki_copy(j).start()
        return carry

    def ki_wait(j, carry):
        ki_copy(j).wait()
        return carry

    lax.fori_loop(0, n_pages, ki_start, 0)
    for cp in group_copies(0, 0):
        cp.start()
    lax.fori_loop(0, n_pages, ki_wait, 0)

    qih = qih_ref[...]
    wcol = wib_ref[:, 0:1]

    def score(kc):
        lg = lax.dot_general(qih, kc.astype(BF16), _NT, preferred_element_type=F32)
        r = jnp.maximum(lg, 0.0) * wcol
        sc = r[0:8]
        for h in range(1, N_IDX_HEADS):
            sc = sc + r[h * 8:(h + 1) * 8]
        return sc

    for g in range(n_groups):
        sc = score(kibuf[g * gk:(g + 1) * gk, :])
        for jj in range(grp):
            score_s[g * grp + jj] = sc[:, jj * LANES:(jj + 1) * LANES]
    lane = lax.broadcasted_iota(I32, (8, LANES), 1)
    qrow = lax.broadcasted_iota(I32, (8, LANES), 0)
    score_s[n_pages] = jnp.where((lane <= qrow) & (lane < t_new), score(kinew_ref[...]), -jnp.inf)
    _select_lane_keys(score_s, bias, n_pages + 1, topk)

    qall = qh_ref[...]
    src_key = lax.broadcasted_iota(I32, (LANES, LANES * N_HEADS_B), 0)
    dst_col = lax.broadcasted_iota(I32, (LANES, LANES * N_HEADS_B), 1)
    spread = jnp.where(dst_col >> 3 == src_key, 1.0, 0.0).astype(BF16)

    def attend(k3, v3, sel_blocks, carry):
        m_old, l_old, acc = carry
        nk = k3.shape[0]
        kf = k3.reshape(nk * N_HEADS_B, HEAD_DIM_B).astype(BF16)
        vf = v3.reshape(nk * N_HEADS_B, HEAD_DIM_B).astype(BF16)
        s = lax.dot_general(qall, kf, _NT, preferred_element_type=F32)
        sel = jnp.concatenate(
            [jnp.dot(jnp.where(blk == 0.0, 1.0, 0.0).astype(BF16), spread, preferred_element_type=F32)
             for blk in sel_blocks], axis=1)
        sel = jnp.concatenate([sel] * N_HEADS_B, axis=0)
        row_head = lax.broadcasted_iota(I32, s.shape, 0) >> 3
        col_head = lax.broadcasted_iota(I32, s.shape, 1) & (N_HEADS_B - 1)
        s = jnp.where((sel > 0.5) & (row_head == col_head), s, NEG_BIG)
        m_new = jnp.maximum(m_old, jnp.max(s, axis=1, keepdims=True))
        alpha = jnp.exp(m_old - m_new)
        p = jnp.exp(s - m_new)
        l_new = alpha * l_old + jnp.sum(p, axis=1, keepdims=True)
        pv = jnp.dot(p.astype(BF16), vf, preferred_element_type=F32)
        return m_new, l_new, acc * alpha + pv

    def group(g, carry):
        slot = g % 2
        for cp in group_copies(g, slot):
            cp.wait()

        @pl.when(g + 1 < n_groups)
        def _():
            for cp in group_copies(g + 1, 1 - slot):
                cp.start()
        return attend(kbuf[slot], vbuf[slot], [bias[g * grp + jj] for jj in range(grp)], carry)

    init = (jnp.full((64, 1), NEG_BIG, F32), jnp.zeros((64, 1), F32), jnp.zeros((64, HEAD_DIM_B), F32))
    carry = lax.fori_loop(0, n_groups, group, init)
    _, l_fin, acc = attend(knew_ref[...], vnew_ref[...], [bias[n_pages]], carry)
    o_ref[...] = acc / l_fin


def _dsa_sample(q, k_new, v_new, qi, wi, ki_new, pool_k, pool_v, pool_ki, page_table):
    db, t, _ = q.shape
    n_pages = page_table.shape[1]
    page = pool_ki.shape[1]
    topk = min(TOPK_MAX, (n_pages * page + t) // 4)
    nj = n_pages + 1
    gk = SAMPLE_GROUP_PAGES * page

    def head_major(a):
        return jnp.pad(a.swapaxes(1, 2), ((0, 0), (0, 0), (0, 8 - t), (0, 0)))

    qih = head_major(qi.reshape(db, t, N_IDX_HEADS, IDX_DIM)).reshape(db, 64, IDX_DIM).astype(BF16)
    wib = jnp.broadcast_to(head_major(wi.reshape(db, t, N_IDX_HEADS, 1)).reshape(db, 64, 1), (db, 64, LANES))
    qh = head_major(q.reshape(db, t, N_HEADS_B, HEAD_DIM_B) * (HEAD_DIM_B ** -0.5))
    qh = qh.reshape(db, 64, HEAD_DIM_B).astype(BF16)
    assert n_pages % SAMPLE_GROUP_PAGES == 0
    pad_rows = lambda a: jnp.pad(a, ((0, 0), (0, page - t)) + ((0, 0),) * (a.ndim - 2))
    seq3 = lambda b, pt: (b, 0, 0)
    seq4 = lambda b, pt: (b, 0, 0, 0)
    hbm = pl.BlockSpec(memory_space=pl.ANY)
    kv_spec = pl.BlockSpec((None, page, N_HEADS_B, HEAD_DIM_B), seq4)
    o = pl.pallas_call(
        functools.partial(_dsa_sample_body, topk=topk, n_pages=n_pages, t_new=t, page=page),
        grid_spec=pltpu.PrefetchScalarGridSpec(
            num_scalar_prefetch=1, grid=(db,),
            in_specs=[pl.BlockSpec((None, 64, IDX_DIM), seq3), pl.BlockSpec((None, 64, LANES), seq3),
                      pl.BlockSpec((None, 64, HEAD_DIM_B), seq3),
                      pl.BlockSpec((None, page, IDX_DIM), seq3), kv_spec, kv_spec, hbm, hbm, hbm],
            out_specs=pl.BlockSpec((None, 64, HEAD_DIM_B), seq3),
            scratch_shapes=[pltpu.VMEM((n_pages * page, IDX_DIM), F32),
                            pltpu.VMEM((2, gk, N_HEADS_B, HEAD_DIM_B), F32),
                            pltpu.VMEM((2, gk, N_HEADS_B, HEAD_DIM_B), F32),
                            pltpu.VMEM((nj, 8, LANES), F32), pltpu.VMEM((nj, 8, LANES), F32),
                            pltpu.SemaphoreType.DMA((3,))]),
        out_shape=jax.ShapeDtypeStruct((db, 64, HEAD_DIM_B), F32),
        compiler_params=_params(1, 48), name="dsa_sample")(
            page_table, qih, wib, qh, pad_rows(ki_new), pad_rows(k_new), pad_rows(v_new),
            pool_ki, pool_k, pool_v)
    o = o.reshape(db, N_HEADS_B, 8, HEAD_DIM_B)[:, :, :t, :]
    return o.swapaxes(1, 2).reshape(db, t, N_HEADS_B * HEAD_DIM_B)


def _pool_out_body(u_ref, pre_ref, battn_ref, x_ref, wpool_ref, scale_ref, wout_ref,
                   x1_ref, hist_ref, ext, *, pos0):
    tt = u_ref.shape[0]
    ti = pl.program_id(1)
    h0 = POOL_HIST

    @pl.when(ti == 0)
    def _():
        ext[0:h0, :] = pre_ref[...]

    ext[h0:h0 + tt, :] = u_ref[...]
    pos = pos0 + ti * tt + lax.broadcasted_iota(I32, (tt, LANES), 0)
    outs = []
    for g, w in enumerate(POOL_WINDOWS):
        cols = slice(g * LANES, (g + 1) * LANES)
        cur = ext[h0:h0 + tt, cols]
        s = cur
        for back in range(1, w):
            s = s + ext[h0 - back:h0 - back + tt, cols]
        cnt = jnp.minimum(pos + 1, w).astype(F32)
        d = (s / cnt - cur).astype(BF16)
        outs.append(jnp.dot(d, wpool_ref[g], preferred_element_type=F32) * scale_ref[:, cols])
    a = jnp.concatenate(outs, axis=1).astype(BF16)
    y = jnp.dot(a, wout_ref[0:512, :], preferred_element_type=F32)
    y = y + jnp.dot(battn_ref[...].astype(BF16), wout_ref[512:1024, :], preferred_element_type=F32)
    x1_ref[...] = x_ref[...] + y
    new_hist = ext[tt:tt + h0, :]
    ext[0:h0, :] = new_hist
    hist_ref[...] = new_hist


def _pool_out(u, prefix, battn, x, w_pool, pool_scale, w_out, pos0):
    bsz, t, _ = u.shape
    d = x.shape[-1]
    tt = 256 if t % 256 == 0 else t
    pre = jnp.pad(prefix, ((0, 0), (POOL_HIST - prefix.shape[1], 0), (0, 0)))
    tile = lambda b, i: (b, i, 0)
    perb = lambda b, i: (b, 0, 0)
    x1, hist = pl.pallas_call(
        functools.partial(_pool_out_body, pos0=pos0),
        grid=(bsz, t // tt),
        in_specs=[pl.BlockSpec((None, tt, 512), tile), pl.BlockSpec((None, POOL_HIST, 512), perb),
                  pl.BlockSpec((None, tt, 512), tile), pl.BlockSpec((None, tt, d), tile),
                  pl.BlockSpec((4, LANES, LANES), lambda b, i: (0, 0, 0)),
                  pl.BlockSpec((1, 512), lambda b, i: (0, 0)),
                  pl.BlockSpec((1024, d), lambda b, i: (0, 0))],
        out_specs=[pl.BlockSpec((None, tt, d), tile), pl.BlockSpec((None, POOL_HIST, 512), perb)],
        out_shape=[jax.ShapeDtypeStruct((bsz, t, d), F32), jax.ShapeDtypeStruct((bsz, POOL_HIST, 512), F32)],
        scratch_shapes=[pltpu.VMEM((POOL_HIST + tt, 512), F32)],
        compiler_params=_params(2, 32), name="pool_outproj")(
            u, pre, battn, x, w_pool.astype(BF16), pool_scale.reshape(1, 512), w_out.astype(BF16))
    return x1, hist[:, 1:, :]


def _conv_body(x_ref, pre_ref, g_ref, win_ref, bin_ref, wdw_ref, bdw_ref, lng_ref, lnb_ref, wout_ref,
               x1_ref, hist_ref, ext, ybuf):
    tt, d = x_ref.shape
    ti = pl.program_id(1)
    h0 = CONV_HIST

    @pl.when(ti == 0)
    def _():
        ext[0:h0, :] = pre_ref[...]

    x = x_ref[...]
    hb = _rms(x, g_ref[...]).astype(BF16)
    a = jnp.dot(hb, win_ref[...], preferred_element_type=F32) + bin_ref[...]
    ext[h0:h0 + tt, :] = a[:, :d] * _sigmoid(a[:, d:])
    rc = min(tt, 128)
    first = h0 - (CONV_WIDTH - 1)
    for r0 in range(0, tt, rc):
        for c0 in range(0, d, LANES):
            cols = slice(c0, c0 + LANES)
            y = jnp.broadcast_to(bdw_ref[:, cols], (rc, LANES))
            for j in range(CONV_WIDTH):
                y = y + ext[first + r0 + j:first + r0 + j + rc, cols] * wdw_ref[j:j + 1, cols]
            ybuf[r0:r0 + rc, cols] = y
    y = ybuf[...]
    mu = jnp.mean(y, axis=-1, keepdims=True)
    yc = y - mu
    var = jnp.mean(yc * yc, axis=-1, keepdims=True)
    yn = yc * lax.rsqrt(var + EPS) * lng_ref[...] + lnb_ref[...]
    act = (yn * _sigmoid(yn)).astype(BF16)
    x1_ref[...] = x + jnp.dot(act, wout_ref[...], preferred_element_type=F32)
    new_hist = ext[tt:tt + h0, :]
    ext[0:h0, :] = new_hist
    hist_ref[...] = new_hist


def _conv_mixer(x, prefix, g, w_in, b_in, w_dw, b_dw, ln_g, ln_b, w_out):
    bsz, t, d = x.shape
    tt = 256 if t % 256 == 0 else t
    pre = jnp.pad(prefix, ((0, 0), (CONV_HIST - prefix.shape[1], 0), (0, 0)))
    wdw = jnp.pad(w_dw, ((0, CONV_HIST - CONV_WIDTH), (0, 0)))
    tile = lambda b, i: (b, i, 0)
    perb = lambda b, i: (b, 0, 0)
    fixed = lambda b, i: (0, 0)
    vec = lambda n: pl.BlockSpec((1, n), fixed)
    x1, hist = pl.pallas_call(
        _conv_body,
        grid=(bsz, t // tt),
        in_specs=[pl.BlockSpec((None, tt, d), tile), pl.BlockSpec((None, CONV_HIST, d), perb), vec(d),
                  pl.BlockSpec((d, 2 * d), fixed), vec(2 * d), pl.BlockSpec((CONV_HIST, d), fixed), vec(d),
                  vec(d), vec(d), pl.BlockSpec((d, d), fixed)],
        out_specs=[pl.BlockSpec((None, tt, d), tile), pl.BlockSpec((None, CONV_HIST, d), perb)],
        out_shape=[jax.ShapeDtypeStruct((bsz, t, d), F32), jax.ShapeDtypeStruct((bsz, CONV_HIST, d), F32)],
        scratch_shapes=[pltpu.VMEM((CONV_HIST + tt, d), F32), pltpu.VMEM((tt, d), F32)],
        compiler_params=_params(2, 40), name="conv_mixer")(
            x, pre, g.reshape(1, d), w_in.astype(BF16), b_in.reshape(1, 2 * d), wdw, b_dw.reshape(1, d),
            ln_g.reshape(1, d), ln_b.reshape(1, d), w_out.astype(BF16))
    return x1, hist[:, CONV_HIST - (CONV_WIDTH - 1):, :]


def _memkv_body(mem_ref, g_ref, wk_ref, wv_ref, k_ref, v_ref, kflat_ref, vflat_ref):
    mb = _rms(mem_ref[...], g_ref[...]).astype(BF16)
    k = jnp.dot(mb, wk_ref[...], preferred_element_type=F32)
    v = jnp.dot(mb, wv_ref[...], preferred_element_type=F32)
    kflat_ref[...] = k
    vflat_ref[...] = v
    hd = k_ref.shape[-1]
    for h in range(N_HEADS_X):
        k_ref[:, h, :] = k[:, h * hd:(h + 1) * hd]
        v_ref[:, h, :] = v[:, h * hd:(h + 1) * hd]


def _mem_kv(mem, g_mem, w_ck, w_cv):
    bsz, m, d = mem.shape
    depth = g_mem.shape[0]
    hd = d // N_HEADS_X
    out = jax.ShapeDtypeStruct((depth, bsz, m, N_HEADS_X, hd), F32)
    flat = jax.ShapeDtypeStruct((depth, bsz, m, d), F32)
    wspec = pl.BlockSpec((None, d, d), lambda l, b: (l, 0, 0))
    ospec = pl.BlockSpec((None, None, m, N_HEADS_X, hd), lambda l, b: (l, b, 0, 0, 0))
    fspec = pl.BlockSpec((None, None, m, d), lambda l, b: (l, b, 0, 0))
    return pl.pallas_call(
        _memkv_body, grid=(depth, bsz),
        in_specs=[pl.BlockSpec((None, m, d), lambda l, b: (b, 0, 0)),
                  pl.BlockSpec((None, 1, d), lambda l, b: (l, 0, 0)), wspec, wspec],
        out_specs=[ospec, ospec, fspec, fspec], out_shape=[out, out, flat, flat],
        compiler_params=_params(2, 32), name="mem_kv")(
            mem, g_mem.reshape(depth, 1, d), w_ck.astype(BF16), w_cv.astype(BF16))


def _cross_body(x_ref, g_ref, wq_ref, mk_ref, mv_ref, wo_ref, o_ref):
    x = x_ref[...]
    d = x.shape[-1]
    hd = d // N_HEADS_X
    hb = _rms(x, g_ref[...]).astype(BF16)
    q = (jnp.dot(hb, wq_ref[...], preferred_element_type=F32) * (hd ** -0.5)).astype(BF16)
    outs = []
    for h in range(N_HEADS_X):
        cols = slice(h * hd, (h + 1) * hd)
        head = (slice(None), cols) if len(mk_ref.shape) == 2 else (slice(None), h, slice(None))
        s = lax.dot_general(q[:, cols], mk_ref[head].astype(BF16), _NT, preferred_element_type=F32)
        p = jnp.exp(s - jnp.max(s, axis=-1, keepdims=True))
        l = jnp.sum(p, axis=-1, keepdims=True)
        o = jnp.dot(p.astype(BF16), mv_ref[head].astype(BF16), preferred_element_type=F32)
        outs.append(o / l)
    o = jnp.concatenate(outs, axis=1).astype(BF16)
    o_ref[...] = x + jnp.dot(o, wo_ref[...], preferred_element_type=F32)


def _cross_attn(x, g, w_cq, mem_k, mem_v, w_co):
    bsz, t, d = x.shape
    m = mem_k.shape[1]
    tt = 256 if t % 256 == 0 else t
    tile = lambda b, i: (b, i, 0)
    fixed = lambda b, i: (0, 0)
    mem_spec = pl.BlockSpec((None,) + mem_k.shape[1:], lambda b, i: (b,) + (0,) * (mem_k.ndim - 1))
    return pl.pallas_call(
        _cross_body, grid=(bsz, t // tt),
        in_specs=[pl.BlockSpec((None, tt, d), tile), pl.BlockSpec((1, d), fixed), pl.BlockSpec((d, d), fixed),
                  mem_spec, mem_spec, pl.BlockSpec((d, d), fixed)],
        out_specs=pl.BlockSpec((None, tt, d), tile),
        out_shape=jax.ShapeDtypeStruct((bsz, t, d), F32),
        compiler_params=_params(2, 32), name="cross_attn")(
            x, g.reshape(1, d), w_cq.astype(BF16), mem_k, mem_v, w_co.astype(BF16))


def _store_row_tiles(ref, x):
    for s in range(ref.shape[1]):
        ref[:, s, :] = x[:, s * LANES:(s + 1) * LANES]


def _load_row_tiles(ref):
    return jnp.concatenate([ref[:, s, :] for s in range(ref.shape[1])], axis=1)


def _router_body(x_ref, g_ref, wr_ref, br_ref, h_ref, eid_ref, gate_ref, rank_ref, cnt_ref, run):
    tm = x_ref.shape[0]
    step = pl.program_id(0)

    @pl.when(step == 0)
    def _():
        run[...] = jnp.zeros(run.shape, F32)

    h = _rms(x_ref[...], g_ref[...])
    _store_row_tiles(h_ref, h)
    lg = jnp.dot(h, wr_ref[...], preferred_element_type=F32, precision=lax.Precision.HIGHEST) + br_ref[...]
    lane = lax.broadcasted_iota(I32, (tm, LANES), 1).astype(F32)
    neg = -jnp.inf

    def first_argmax(v):
        mx = jnp.max(v, axis=1, keepdims=True)
        return mx, jnp.min(jnp.where(v == mx, lane, 1e9), axis=1, keepdims=True)

    is_g = lane < N_GROUPS
    gmax, g_sel = first_argmax(jnp.where(is_g, lg, neg))
    g_prob = 1.0 / jnp.sum(jnp.where(is_g, jnp.exp(lg - gmax), 0.0), axis=1, keepdims=True)
    lo = N_GROUPS + g_sel * EXPERTS_PER_GROUP
    el = jnp.where((lane >= lo) & (lane < lo + EXPERTS_PER_GROUP), lg, neg)
    v1, i1 = first_argmax(el)
    v2, i2 = first_argmax(jnp.where(lane == i1, neg, el))
    e2 = jnp.exp(v2 - v1)
    gate1 = g_prob / (1.0 + e2)
    gate2 = g_prob * e2 / (1.0 + e2)
    id1 = i1 - N_GROUPS
    id2 = i2 - N_GROUPS
    oh1 = jnp.where(lane == id1, 1.0, 0.0)
    oh2 = jnp.where(lane == id2, 1.0, 0.0)
    r = lax.broadcasted_iota(I32, (tm, tm), 0)
    c = lax.broadcasted_iota(I32, (tm, tm), 1)
    strict = jnp.where(c < r, 1.0, 0.0).astype(BF16)
    before = jnp.dot(strict, (oh1 + oh2).astype(BF16), preferred_element_type=F32) + run[...]
    rank1 = jnp.sum(oh1 * before, axis=1, keepdims=True)
    rank2 = jnp.sum(oh2 * before, axis=1, keepdims=True)
    run[...] = run[...] + jnp.sum(oh1 + oh2, axis=0, keepdims=True)
    two = lax.broadcasted_iota(I32, (tm, 2), 1)
    eid_ref[...] = jnp.where(two == 0, id1, id2).astype(I32)
    gate_ref[...] = jnp.where(two == 0, gate1, gate2)
    rank_ref[...] = jnp.where(two == 0, rank1, rank2).astype(I32)
    cnt_ref[...] = run[...].astype(I32)


def _router(x2d, g, w_rg, b_rg, w_re, b_re):
    n, d = x2d.shape
    tm = _token_tile(n)
    w_r = jnp.concatenate([w_rg, w_re.transpose(1, 0, 2).reshape(d, N_EXPERTS)], axis=1)
    w_r = jnp.pad(w_r, ((0, 0), (0, LANES - w_r.shape[1])))
    b_r = jnp.pad(jnp.concatenate([b_rg, b_re.reshape(N_EXPERTS)]), (0, LANES - N_GROUPS - N_EXPERTS))
    row = lambda i: (i, 0)
    fixed = lambda i: (0, 0)
    pair = pl.BlockSpec((tm, 2), row)
    return pl.pallas_call(
        _router_body, grid=(n // tm,),
        in_specs=[pl.BlockSpec((tm, d), row), pl.BlockSpec((1, d), fixed), pl.BlockSpec((d, LANES), fixed),
                  pl.BlockSpec((1, LANES), fixed)],
        out_specs=[pl.BlockSpec((tm, d // LANES, LANES), lambda i: (i, 0, 0)), pair, pair, pair,
                   pl.BlockSpec((1, LANES), fixed)],
        out_shape=[jax.ShapeDtypeStruct((n, d // LANES, LANES), F32), jax.ShapeDtypeStruct((n, 2), I32),
                   jax.ShapeDtypeStruct((n, 2), F32), jax.ShapeDtypeStruct((n, 2), I32),
                   jax.ShapeDtypeStruct((1, LANES), I32)],
        scratch_shapes=[pltpu.VMEM((1, LANES), F32)],
        compiler_params=_params(1, 32), name="moe_router")(x2d, g.reshape(1, d), w_r, b_r.reshape(1, LANES))


def _dispatch_body(slot_ref, pend_ref, h_ref, xs_hbm, zeros, sem, *, tm, tb, nb):
    step = pl.program_id(0)

    @pl.when(step == 0)
    def _():
        zeros[...] = jnp.zeros(zeros.shape, F32)
        n_used = pend_ref[N_EXPERTS - 1] // tb

        def zero_copy(blk):
            return pltpu.make_async_copy(zeros, xs_hbm.at[pl.ds(pl.multiple_of(blk * tb, tb), tb)], sem.at[0])

        def seg_last_block(e):
            start = jnp.where(e == 0, 0, pend_ref[jnp.maximum(e - 1, 0)])
            return pend_ref[e] > start, jnp.maximum(pend_ref[e] // tb - 1, 0)

        def each(fn):
            def seg(e, carry):
                nonempty, blk = seg_last_block(e)

                @pl.when(nonempty)
                def _():
                    fn(zero_copy(blk))
                return carry
            lax.fori_loop(0, N_EXPERTS, seg, 0)

            def tail(blk, carry):
                @pl.when(blk >= n_used)
                def _():
                    fn(zero_copy(blk))
                return carry
            lax.fori_loop(0, nb, tail, 0)

        each(lambda cp: cp.start())
        each(lambda cp: cp.wait())

    base = step * tm

    def row_copy(r, j):
        return pltpu.make_async_copy(h_ref.at[r], xs_hbm.at[slot_ref[2 * (base + r) + j]], sem.at[1])

    def rstart(r, carry):
        row_copy(r, 0).start()
        row_copy(r, 1).start()
        return carry
    lax.fori_loop(0, tm, rstart, 0)

    def rwait(r, carry):
        row_copy(r, 0).wait()
        row_copy(r, 1).wait()
        return carry
    lax.fori_loop(0, tm, rwait, 0)


def _dispatch(h2, slot_flat, pend, n_rows, tb):
    n, s, l = h2.shape
    tm = _token_tile(n)
    return pl.pallas_call(
        functools.partial(_dispatch_body, tm=tm, tb=tb, nb=n_rows // tb),
        grid_spec=pltpu.PrefetchScalarGridSpec(
            num_scalar_prefetch=2, grid=(n // tm,),
            in_specs=[pl.BlockSpec((tm, s, l), lambda i, sl, pe: (i, 0, 0))],
            out_specs=pl.BlockSpec(memory_space=pl.ANY),
            scratch_shapes=[pltpu.VMEM((tb, s, l), F32), pltpu.SemaphoreType.DMA((2,))]),
        out_shape=jax.ShapeDtypeStruct((n_rows, s, l), F32),
        compiler_params=_params(1), name="moe_dispatch")(slot_flat, pend, h2)


def _expert_body(bexp_ref, nvalid_ref, xs_ref, w1_ref, w3_ref, w2_ref, ys_ref, w1b, w3b, w2b):
    i = pl.program_id(0)
    live = i < nvalid_ref[0]
    new_expert = (i == 0) | (bexp_ref[i] != bexp_ref[jnp.maximum(i - 1, 0)])

    @pl.when(live & new_expert)
    def _():
        w1b[...] = w1_ref[...].astype(BF16)
        w3b[...] = w3_ref[...].astype(BF16)
        w2b[...] = w2_ref[...].astype(BF16)

    @pl.when(live)
    def _():
        xb = _load_row_tiles(xs_ref).astype(BF16)
        a = jnp.dot(xb, w1b[...], preferred_element_type=F32)
        b = jnp.dot(xb, w3b[...], preferred_element_type=F32)
        hmid = (a * _sigmoid(a) * b).astype(BF16)
        _store_row_tiles(ys_ref, jnp.dot(hmid, w2b[...], preferred_element_type=F32))

    @pl.when(i >= nvalid_ref[0])
    def _():
        ys_ref[...] = jnp.zeros(ys_ref.shape, F32)


def _experts(xs, blk_exp, nvalid, w1, w3, w2, layer, tb):
    p, s, l = xs.shape
    d = s * l
    ff = w1.shape[-1]
    nb = p // tb
    rows_in = lambda i, be, nv: (jnp.minimum(i, jnp.maximum(nv[0] - 1, 0)), 0, 0)
    wmap = lambda i, be, nv: (layer, be[i], 0, 0)
    return pl.pallas_call(
        _expert_body,
        grid_spec=pltpu.PrefetchScalarGridSpec(
            num_scalar_prefetch=2, grid=(nb,),
            in_specs=[pl.BlockSpec((tb, s, l), rows_in), pl.BlockSpec((None, None, d, ff), wmap),
                      pl.BlockSpec((None, None, d, ff), wmap), pl.BlockSpec((None, None, ff, d), wmap)],
            out_specs=pl.BlockSpec((tb, s, l), lambda i, be, nv: (i, 0, 0)),
            scratch_shapes=[pltpu.VMEM((d, ff), BF16), pltpu.VMEM((d, ff), BF16), pltpu.VMEM((ff, d), BF16)]),
        out_shape=jax.ShapeDtypeStruct((p, s, l), F32),
        compiler_params=_params(1, 40), name="moe_experts")(blk_exp, nvalid, xs, w1, w3, w2)


def _combine_body(slot_ref, x_ref, gate_ref, gf_ref, ys_hbm, o_ref, buf, sem, *, final_norm):
    tm = x_ref.shape[0]
    base = pl.program_id(0) * tm * 2

    def row_copy(r, j):
        return pltpu.make_async_copy(ys_hbm.at[slot_ref[base + 2 * r + j]], buf.at[j, r], sem.at[0])

    def start(r, carry):
        row_copy(r, 0).start()
        row_copy(r, 1).start()
        return carry
    lax.fori_loop(0, tm, start, 0)

    def wait(r, carry):
        row_copy(r, 0).wait()
        row_copy(r, 1).wait()
        return carry
    lax.fori_loop(0, tm, wait, 0)

    gate = gate_ref[...]
    y = x_ref[...] + _load_row_tiles(buf.at[0]) * gate[:, 0:1] + _load_row_tiles(buf.at[1]) * gate[:, 1:2]
    if final_norm:
        y = _rms(y, gf_ref[...])
    o_ref[...] = y


def _combine(x2d, gates, slot_flat, ys, g_final):
    n, d = x2d.shape
    tm = 256 if n % 256 == 0 else n
    final_norm = g_final is not None
    gf = (g_final if final_norm else jnp.ones((d,), F32)).reshape(1, d)
    row = lambda i, s: (i, 0)
    return pl.pallas_call(
        functools.partial(_combine_body, final_norm=final_norm),
        grid_spec=pltpu.PrefetchScalarGridSpec(
            num_scalar_prefetch=1, grid=(n // tm,),
            in_specs=[pl.BlockSpec((tm, d), row), pl.BlockSpec((tm, 2), row),
                      pl.BlockSpec((1, d), lambda i, s: (0, 0)), pl.BlockSpec(memory_space=pl.ANY)],
            out_specs=pl.BlockSpec((tm, d), row),
            scratch_shapes=[pltpu.VMEM((2, tm, d // LANES, LANES), F32), pltpu.SemaphoreType.DMA((1,))]),
        out_shape=jax.ShapeDtypeStruct((n, d), F32),
        compiler_params=_params(1, 32), name="moe_combine")(slot_flat, x2d, gates, gf, ys)


def _moe(x2d, g, w_rg, b_rg, w_re, b_re, w1, w3, w2, layer, g_final):
    n, _ = x2d.shape
    tb = 256 if n >= 2048 else 128
    h2, eid, gates, rank, counts = _router(x2d, g, w_rg, b_rg, w_re, b_re)
    counts = counts[0, :N_EXPERTS]
    pcounts = (counts + tb - 1) // tb * tb
    pend = jnp.cumsum(pcounts)
    pstart = pend - pcounts
    slot_flat = (pstart[eid] + rank).reshape(2 * n).astype(I32)
    nb = -(-(2 * n + N_EXPERTS * (tb - 1)) // tb)
    blk_start = jnp.arange(nb, dtype=I32) * tb
    blk_exp = jnp.minimum(jnp.sum(pend[None, :] <= blk_start[:, None], axis=1), N_EXPERTS - 1).astype(I32)
    nvalid = (pend[-1:] // tb).astype(I32)
    xs = _dispatch(h2, slot_flat, pend.astype(I32), nb * tb, tb)
    ys = _experts(xs, blk_exp, nvalid, w1, w3, w2, layer, tb)
    return _combine(x2d, gates, slot_flat, ys, g_final)


def _trunk(x, pos0, pool_prefix, conv_prefix, mem_k, mem_v, attn_fn, p, we):
    bsz, t, d = x.shape
    n = bsz * t
    proj = _inproj(x.reshape(n, d), p["g_mix"][0], p["w_in_even"][0], with_vt=attn_fn == "prompt")
    u, qb, k, v, kb, qib, ki, kib, wi = proj[:9]
    r3 = lambda a: a.reshape(bsz, t, a.shape[-1])
    if attn_fn == "prompt":
        vt = proj[9].reshape(bsz, t // KEY_CHUNK, 512, KEY_CHUNK)
        battn = _dsa_prompt(r3(qib), r3(wi), r3(kib), r3(qb), r3(kb), vt)
    else:
        r4 = lambda a: a.reshape(bsz, t, N_HEADS_B, HEAD_DIM_B)
        battn = attn_fn(r3(qb).astype(F32), r4(k), r4(v), r3(qib).astype(F32), r3(wi), r3(ki))
    x, new_pool = _pool_out(r3(u), pool_prefix[0], battn, x, p["w_pool"][0], p["pool_scale"][0],
                            p["w_out_even"][0], pos0)
    x = _cross_attn(x, p["g_cross"][0], p["w_cq"][0], mem_k[0], mem_v[0], p["w_co"][0])
    x = _moe(x.reshape(n, d), p["g_ffn"][0], p["w_rg"][0], p["b_rg"][0], p["w_re"][0], p["b_re"][0],
             we[0], we[1], we[2], 0, None).reshape(bsz, t, d)
    x, new_conv = _conv_mixer(x, conv_prefix[0], p["g_mix"][1], p["w_conv_in"][0], p["b_conv_in"][0],
                              p["w_dw"][0], p["b_dw"][0], p["ln_g"][0], p["ln_b"][0], p["w_conv_out"][0])
    x = _cross_attn(x, p["g_cross"][1], p["w_cq"][1], mem_k[1], mem_v[1], p["w_co"][1])
    y = _moe(x.reshape(n, d), p["g_ffn"][1], p["w_rg"][1], p["b_rg"][1], p["w_re"][1], p["b_re"][1],
             we[0], we[1], we[2], 1, p["g_final"]).reshape(bsz, t, d)
    hk = lambda a: a.reshape(1, bsz, t, N_HEADS_B, HEAD_DIM_B)
    return y, hk(k), hk(v), ki.reshape(1, bsz, t, IDX_DIM), new_pool[None], new_conv[None]


def kernel(x_prompt, x_sample, mem_prompt, cache_attn_k, cache_attn_v, cache_idx_k, page_table,
           state_pool, state_conv, cache_mem_k, cache_mem_v, g_mix, g_cross, g_mem, g_ffn, g_final,
           w_in_even, w_pool, pool_scale, w_out_even, w_conv_in, b_conv_in, w_dw, b_dw, ln_g, ln_b,
           w_conv_out, w_cq, w_ck, w_cv, w_co, w_rg, b_rg, w_re, b_re, w_e1, w_e3, w_e2):
    p = {"g_mix": g_mix, "g_cross": g_cross, "g_ffn": g_ffn, "g_final": g_final,
         "w_in_even": w_in_even, "w_pool": w_pool, "pool_scale": pool_scale, "w_out_even": w_out_even,
         "w_conv_in": w_conv_in, "b_conv_in": b_conv_in, "w_dw": w_dw, "b_dw": b_dw, "ln_g": ln_g,
         "ln_b": ln_b, "w_conv_out": w_conv_out, "w_cq": w_cq, "w_co": w_co, "w_rg": w_rg, "b_rg": b_rg,
         "w_re": w_re, "b_re": b_re}
    we = (w_e1, w_e3, w_e2)
    bsz, _, d = x_prompt.shape

    def sample_attn(q, k, v, qi, wi, ki):
        return _dsa_sample(q, k, v, qi, wi, ki, cache_attn_k[0], cache_attn_v[0], cache_idx_k[0], page_table)

    past = page_table.shape[1] * cache_idx_k.shape[2]
    y_s, nk_s, nv_s, nki_s, npool_s, nconv_s = _trunk(
        x_sample, past, state_pool, state_conv, cache_mem_k, cache_mem_v, sample_attn, p, we)

    mem_k_p, mem_v_p, mem_k_flat, mem_v_flat = _mem_kv(mem_prompt, g_mem, w_ck, w_cv)
    zeros_pool = jnp.zeros((1, bsz, POOL_HIST - 1, 512), F32)
    zeros_conv = jnp.zeros((1, bsz, CONV_WIDTH - 1, d), F32)
    y_p, nk_p, nv_p, nki_p, npool_p, nconv_p = _trunk(
        x_prompt, 0, zeros_pool, zeros_conv, mem_k_flat, mem_v_flat, "prompt", p, we)

    return (y_p, y_s, nk_p, nv_p, nki_p, npool_p, nconv_p, mem_k_p, mem_v_p,
            nk_s, nv_s, nki_s, npool_s, nconv_s)
```

```python
import functools
import math

import jax
import jax.numpy as jnp
from jax import lax
from jax.experimental import pallas as pl
from jax.experimental.pallas import tpu as pltpu

F32 = jnp.float32
BF16 = jnp.bfloat16
I32 = jnp.int32

EPS = 1e-6
POOL_WINDOWS = (2, 4, 8, 16)
POOL_HIST = 16
CONV_WIDTH = 31
CONV_HIST = 32
N_HEADS_B = 8
HEAD_DIM_B = 64
N_IDX_HEADS = 8
IDX_DIM = 64
TOPK_MAX = 256
Q_BLOCK = 128
KEY_CHUNK = 256
N_HEADS_X = 4
N_GROUPS = 4
EXPERTS_PER_GROUP = 8
N_EXPERTS = N_GROUPS * EXPERTS_PER_GROUP
LANES = 128
INT_MIN = -(2 ** 31)
NEG_BIG = -1e30
MOST_NEGATIVE = -3.4028234663852886e38
MIB = 1024 * 1024

_NT = (((1,), (1,)), ((), ()))


def _params(n_axes, vmem_mib=None, **kw):
    if vmem_mib is not None:
        kw["vmem_limit_bytes"] = vmem_mib * MIB
    return pltpu.CompilerParams(dimension_semantics=("arbitrary",) * n_axes, **kw)


def _rms(x, g):
    return x * lax.rsqrt(jnp.mean(x * x, axis=-1, keepdims=True) + EPS) * g


def _sigmoid(x):
    return 1.0 / (1.0 + jnp.exp(-x))


def _float_of_rank(u):
    key = u ^ INT_MIN
    return pltpu.bitcast(jnp.where(key >= 0, key, key ^ 0x7FFFFFFF), F32)


def _topk_threshold(count_ge, shape, topk):
    def bit_step(it, carry):
        prefix, n_ge = carry
        cand = prefix | (jnp.int32(1) << (31 - it))
        cnt = count_ge(_float_of_rank(cand))
        ok = cnt >= topk
        return jnp.where(ok, cand, prefix), jnp.where(ok, cnt, n_ge)

    prefix, n_ge = lax.fori_loop(0, 32, bit_step, (jnp.zeros(shape, I32), jnp.full(shape, 2.0 ** 30, F32)))
    lowest_finite_rank = INT_MIN + 0x00800000
    has_k = (prefix ^ INT_MIN) >= lowest_finite_rank
    return _float_of_rank(prefix), has_k, n_ge


def _token_tile(n):
    return 512 if n % 512 == 0 else n


def _inproj_body(x_ref, g_ref, w_ref, *refs, with_vt):
    if with_vt:
        wvt_ref, refs = refs[0], refs[1:]
    u_ref, q_ref, k_ref, v_ref, kb_ref, qi_ref, ki_ref, kib_ref, wi_ref = refs[:9]
    hb = _rms(x_ref[...], g_ref[...]).astype(BF16)

    def proj(c0, c1):
        return jnp.dot(hb, w_ref[:, c0:c1], preferred_element_type=F32)

    u_ref[...] = proj(0, 512)
    q_ref[...] = proj(512, 1024).astype(BF16)
    k = proj(1024, 1536)
    v = proj(1536, 2048)
    for h in range(N_HEADS_B):
        k_ref[:, h, :] = k[:, h * HEAD_DIM_B:(h + 1) * HEAD_DIM_B]
        v_ref[:, h, :] = v[:, h * HEAD_DIM_B:(h + 1) * HEAD_DIM_B]
    kb_ref[...] = k.astype(BF16)
    qi_ref[...] = proj(2048, 2560).astype(BF16)
    ki = proj(2560, 2688)[:, :IDX_DIM]
    ki_ref[...] = ki
    kib_ref[...] = ki.astype(BF16)
    wi_ref[...] = proj(2688, 2816)[:, :N_IDX_HEADS] * (N_IDX_HEADS ** -0.5)
    if with_vt:
        vt_ref = refs[9]
        vt = lax.dot_general(wvt_ref[...], hb, _NT, preferred_element_type=F32)
        for c in range(vt_ref.shape[0]):
            vt_ref[c] = vt[:, c * KEY_CHUNK:(c + 1) * KEY_CHUNK].astype(BF16)


def _pack_w_in(w_in):
    d = w_in.shape[0]
    main = w_in[:, :2560]
    ki = jnp.pad(w_in[:, 2560:2624], ((0, 0), (0, 64)))
    wi = jnp.pad(w_in[:, 2624:2632], ((0, 0), (0, 120)))
    del d
    return jnp.concatenate([main, ki, wi], axis=1).astype(BF16)


def _inproj(x2d, g, w_in, with_vt):
    n, d = x2d.shape
    tm = _token_tile(n)
    w = _pack_w_in(w_in)
    row = lambda i: (i, 0)
    fixed = lambda i: (0, 0)
    in_specs = [pl.BlockSpec((tm, d), row), pl.BlockSpec((1, d), fixed), pl.BlockSpec(w.shape, fixed)]
    args = [x2d, g.reshape(1, d), w]
    if with_vt:
        wvt = w_in[:, 1536:2048].T.astype(BF16)
        in_specs.append(pl.BlockSpec(wvt.shape, fixed))
        args.append(wvt)
    heads = (n, N_HEADS_B, HEAD_DIM_B)
    shapes = [((n, 512), F32), ((n, 512), BF16), (heads, F32), (heads, F32), ((n, 512), BF16),
              ((n, 512), BF16), ((n, IDX_DIM), F32), ((n, IDX_DIM), BF16), ((n, N_IDX_HEADS), F32)]
    out_shape = [jax.ShapeDtypeStruct(s, t) for s, t in shapes]
    out_specs = [pl.BlockSpec((tm,) + s[1:], (lambda i: (i, 0, 0)) if len(s) == 3 else row) for s, _ in shapes]
    if with_vt:
        cpt = tm // KEY_CHUNK
        out_shape.append(jax.ShapeDtypeStruct((n // KEY_CHUNK, 512, KEY_CHUNK), BF16))
        out_specs.append(pl.BlockSpec((cpt, 512, KEY_CHUNK), lambda i: (i, 0, 0)))
    return pl.pallas_call(
        functools.partial(_inproj_body, with_vt=with_vt),
        grid=(n // tm,), in_specs=in_specs, out_specs=out_specs, out_shape=out_shape,
        compiler_params=_params(1, 48), name="inproj_even")(*args)


def _dsa_prompt_body(qi_ref, wit_ref, ki_ref, q_ref, kb_ref, vt_ref, o_ref, score_s, bias, s_all, p_all, *accs,
                     topk):
    ck, qb = KEY_CHUNK, Q_BLOCK
    i = pl.program_id(1)
    n_keys = (i + 1) * qb
    nch = (n_keys + ck - 1) // ck

    qi_all = qi_ref[...]
    wit = wit_ref[...]
    q_pos = i * qb + lax.broadcasted_iota(I32, (ck, qb), 1)
    k_row = lax.broadcasted_iota(I32, (ck, qb), 0)

    def score_chunk(c, carry):
        off = pl.multiple_of(c * ck, ck)
        kc = ki_ref[pl.ds(off, ck), :]
        lg = lax.dot_general(kc, qi_all, _NT, preferred_element_type=F32)
        sc = jnp.zeros((ck, qb), F32)
        for h in range(N_IDX_HEADS):
            sc = sc + jnp.maximum(lg[:, h * qb:(h + 1) * qb], 0.0) * wit[h:h + 1, :]
        score_s[pl.ds(off, ck), :] = jnp.where(k_row + off <= q_pos, sc, -jnp.inf)
        return carry

    lax.fori_loop(0, nch, score_chunk, 0)

    def count(pred):
        def body(c, cnt):
            off = pl.multiple_of(c * ck, ck)
            ind = jnp.where(pred(score_s[pl.ds(off, ck), :]), 1.0, 0.0)
            return cnt + jnp.sum(ind.reshape(8, ck // 8, qb), axis=0)
        cnt = lax.fori_loop(0, nch, body, jnp.zeros((ck // 8, qb), F32))
        return jnp.sum(cnt, axis=0, keepdims=True)

    thr, has_k, n_ge = _topk_threshold(lambda t: count(lambda s: s >= t), (1, qb), topk)
    tie_excess = jnp.max(jnp.where(has_k & (n_ge > topk), 1.0, 0.0)) > 0.5

    @pl.when(jnp.logical_not(tie_excess))
    def _():
        tsel = jnp.where(has_k, thr, MOST_NEGATIVE)

        def body(c, carry):
            off = pl.multiple_of(c * ck, ck)
            bias[pl.ds(off, ck), :] = jnp.where(score_s[pl.ds(off, ck), :] >= tsel, 0.0, NEG_BIG)
            return carry
        lax.fori_loop(0, nch, body, 0)

    @pl.when(tie_excess)
    def _():
        need = topk - count(lambda s: s > thr)
        thr_k = jnp.where(has_k, thr, -jnp.inf)
        r = lax.broadcasted_iota(I32, (ck, ck), 0)
        cidx = lax.broadcasted_iota(I32, (ck, ck), 1)
        tri = jnp.where(cidx <= r, 1.0, 0.0).astype(BF16)

        def body(c, run):
            off = pl.multiple_of(c * ck, ck)
            sc = score_s[pl.ds(off, ck), :]
            eq = jnp.where(sc == thr, 1.0, 0.0)
            incl = jnp.dot(tri, eq.astype(BF16), preferred_element_type=F32)
            rank = incl - eq + run
            sel = (sc > thr_k) | ((eq > 0.0) & (rank < need))
            bias[pl.ds(off, ck), :] = jnp.where(sel, 0.0, NEG_BIG)
            return run + jnp.sum(eq, axis=0, keepdims=True)
        lax.fori_loop(0, nch, body, jnp.zeros((1, qb), F32))

    qf = q_ref[...].astype(F32)
    lane = lax.broadcasted_iota(I32, (qb, LANES), 1)
    qp = []
    for h in range(N_HEADS_B):
        pair = qf[:, (h // 2) * LANES:(h // 2 + 1) * LANES]
        keep = (lane >= HEAD_DIM_B) if h % 2 else (lane < HEAD_DIM_B)
        qp.append((jnp.where(keep, pair, 0.0) * (HEAD_DIM_B ** -0.5)).astype(BF16))
    for a in accs:
        a[...] = jnp.zeros(a.shape, F32)

    def fold(x, op):
        y = x.reshape(8, ck // 8, qb)
        r = op(op(op(y[0], y[1]), op(y[2], y[3])), op(op(y[4], y[5]), op(y[6], y[7])))
        return jnp.max(r, axis=0, keepdims=True) if op is jnp.maximum else jnp.sum(r, axis=0, keepdims=True)

    def attend_chunk(c, carry):
        off = pl.multiple_of(c * ck, ck)
        b = bias[pl.ds(off, ck), :]
        ms, ls = carry
        new_m, new_l, alphas = [], [], []
        for h in range(N_HEADS_B):
            k2 = kb_ref[pl.ds(off, ck), (h // 2) * LANES:(h // 2 + 1) * LANES]
            s_all[h] = lax.dot_general(k2, qp[h], _NT, preferred_element_type=F32) + b
        for h in range(N_HEADS_B):
            s = s_all[h]
            m_new = jnp.maximum(ms[h], fold(s, jnp.maximum))
            alphas.append(jnp.exp(ms[h] - m_new))
            p = jnp.exp(s - m_new)
            new_l.append(alphas[h] * ls[h] + fold(p, jnp.add))
            new_m.append(m_new)
            p_all[h] = p.astype(BF16)
        for h in range(N_HEADS_B):
            rows = slice(h * HEAD_DIM_B, (h + 1) * HEAD_DIM_B)
            pv = jnp.dot(vt_ref[c, rows, :], p_all[h], preferred_element_type=F32)
            accs[h][...] = accs[h][...] * alphas[h] + pv
        return tuple(new_m), tuple(new_l)

    init = (tuple(jnp.full((1, qb), NEG_BIG, F32) for _ in range(N_HEADS_B)),
            tuple(jnp.zeros((1, qb), F32) for _ in range(N_HEADS_B)))
    _, ls = lax.fori_loop(0, nch, attend_chunk, init)
    o = jnp.concatenate([accs[h][...] / ls[h] for h in range(N_HEADS_B)], axis=0)
    o_ref[...] = o.T.astype(BF16)


def _dsa_prompt(qi, wi, kib, q, kb, vt):
    bsz, t, _ = q.shape
    nblk = t // Q_BLOCK
    topk = min(TOPK_MAX, t // 4)
    wit = wi.reshape(bsz, nblk, Q_BLOCK, N_IDX_HEADS).swapaxes(2, 3)
    qi = qi.reshape(bsz, nblk, Q_BLOCK, N_IDX_HEADS, IDX_DIM).swapaxes(2, 3)
    qi = qi.reshape(bsz, nblk, N_IDX_HEADS * Q_BLOCK, IDX_DIM)
    blk = lambda b, i: (b, i, 0)
    whole = lambda b, i: (b, 0, 0)
    once = pl.Buffered(1)
    return pl.pallas_call(
        functools.partial(_dsa_prompt_body, topk=topk),
        grid=(bsz, nblk),
        in_specs=[
            pl.BlockSpec((None, None, N_IDX_HEADS * Q_BLOCK, IDX_DIM), lambda b, i: (b, i, 0, 0)),
            pl.BlockSpec((None, None, N_IDX_HEADS, Q_BLOCK), lambda b, i: (b, i, 0, 0)),
            pl.BlockSpec((None, t, IDX_DIM), whole, pipeline_mode=once),
            pl.BlockSpec((None, Q_BLOCK, 512), blk),
            pl.BlockSpec((None, t, 512), whole, pipeline_mode=once),
            pl.BlockSpec((None, t // KEY_CHUNK, 512, KEY_CHUNK), lambda b, i: (b, 0, 0, 0),
                         pipeline_mode=once),
        ],
        out_specs=pl.BlockSpec((None, Q_BLOCK, 512), blk),
        out_shape=jax.ShapeDtypeStruct((bsz, t, 512), BF16),
        scratch_shapes=[pltpu.VMEM((t, Q_BLOCK), F32), pltpu.VMEM((t, Q_BLOCK), F32),
                        pltpu.VMEM((N_HEADS_B, KEY_CHUNK, Q_BLOCK), F32),
                        pltpu.VMEM((N_HEADS_B, KEY_CHUNK, Q_BLOCK), BF16)]
        + [pltpu.VMEM((HEAD_DIM_B, Q_BLOCK), F32)] * N_HEADS_B,
        compiler_params=_params(2, 48), name="dsa_prompt")(qi, wit, kib, q, kb, vt)


SAMPLE_GROUP_PAGES = 8


def _select_lane_keys(skey, bias_ref, nj, topk):
    def count(pred):
        def body(jj, cnt):
            return cnt + jnp.where(pred(skey[jj]), 1.0, 0.0)
        cnt = lax.fori_loop(0, nj, body, jnp.zeros((8, LANES), F32))
        return jnp.sum(cnt, axis=1, keepdims=True)

    thr, has_k, n_ge = _topk_threshold(lambda t: count(lambda s: s >= t), (8, 1), topk)
    tie_excess = jnp.max(jnp.where(has_k & (n_ge > topk), 1.0, 0.0)) > 0.5

    @pl.when(jnp.logical_not(tie_excess))
    def _():
        tsel = jnp.where(has_k, thr, MOST_NEGATIVE)

        def body(jj, carry):
            bias_ref[jj] = jnp.where(skey[jj] >= tsel, 0.0, NEG_BIG)
            return carry
        lax.fori_loop(0, nj, body, 0)

    @pl.when(tie_excess)
    def _():
        need = topk - count(lambda s: s > thr)
        thr_k = jnp.where(has_k, thr, -jnp.inf)
        a = lax.broadcasted_iota(I32, (LANES, LANES), 0)
        bcol = lax.broadcasted_iota(I32, (LANES, LANES), 1)
        upper = jnp.where(a <= bcol, 1.0, 0.0).astype(BF16)

        def body(jj, run):
            sc = skey[jj]
            eq = jnp.where(sc == thr, 1.0, 0.0)
            incl = jnp.dot(eq.astype(BF16), upper, preferred_element_type=F32)
            rank = incl - eq + run
            sel = (sc > thr_k) | ((eq > 0.0) & (rank < need))
            bias_ref[jj] = jnp.where(sel, 0.0, NEG_BIG)
            return run + jnp.sum(eq, axis=1, keepdims=True)
        lax.fori_loop(0, nj, body, jnp.zeros((8, 1), F32))


def _dsa_sample_body(pt_ref, qih_ref, wib_ref, qh_ref, kinew_ref, knew_ref, vnew_ref,
                     kipool, kpool, vpool, o_ref, kibuf, kbuf, vbuf, score_s, bias, sem,
                     *, topk, n_pages, t_new, page, layer):
    b = pl.program_id(0)
    grp = SAMPLE_GROUP_PAGES
    n_groups = n_pages // grp
    gk = grp * page

    def ki_copy(j):
        return pltpu.make_async_copy(kipool.at[layer, pt_ref[b, j]], kibuf.at[pl.ds(j * page, page)], sem.at[0])

    def group_copies(g, slot):
        cps = []
        for jj in range(grp):
            pg = pt_ref[b, g * grp + jj]
            dst = pl.ds(jj * page, page)
            cps.append(pltpu.make_async_copy(kpool.at[layer, pg], kbuf.at[slot, dst], sem.at[1 + slot]))
            cps.append(pltpu.make_async_copy(vpool.at[layer, pg], vbuf.at[slot, dst], sem.at[1 + slot]))
        return cps

    def ki_start(j, carry):
        ki_copy(j).start()
        return carry

    def ki_wait(j, carry):
        ki_copy(j).wait()
        return carry

    lax.fori_loop(0, n_pages, ki_start, 0)
    for cp in group_copies(0, 0):
        cp.start()
    lax.fori_loop(0, n_pages, ki_wait, 0)

    qih = qih_ref[...]
    wcol = wib_ref[:, 0:1]

    def score(kc):
        lg = lax.dot_general(qih, kc.astype(BF16), _NT, preferred_element_type=F32)
        r = jnp.maximum(lg, 0.0) * wcol
        sc = r[0:8]
        for h in range(1, N_IDX_HEADS):
            sc = sc + r[h * 8:(h + 1) * 8]
        return sc

    for g in range(n_groups):
        sc = score(kibuf[g * gk:(g + 1) * gk, :])
        for jj in range(grp):
            score_s[g * grp + jj] = sc[:, jj * LANES:(jj + 1) * LANES]
    lane = lax.broadcasted_iota(I32, (8, LANES), 1)
    qrow = lax.broadcasted_iota(I32, (8, LANES), 0)
    score_s[n_pages] = jnp.where((lane <= qrow) & (lane < t_new), score(kinew_ref[...]), -jnp.inf)
    _select_lane_keys(score_s, bias, n_pages + 1, topk)

    qall = qh_ref[...]
    src_key = lax.broadcasted_iota(I32, (LANES, LANES * N_HEADS_B), 0)
    dst_col = lax.broadcasted_iota(I32, (LANES, LANES * N_HEADS_B), 1)
    spread = jnp.where(dst_col >> 3 == src_key, 1.0, 0.0).astype(BF16)

    def attend(k3, v3, sel_blocks, carry):
        m_old, l_old, acc = carry
        nk = k3.shape[0]
        kf = k3.reshape(nk * N_HEADS_B, HEAD_DIM_B).astype(BF16)
        vf = v3.reshape(nk * N_HEADS_B, HEAD_DIM_B).astype(BF16)
        s = lax.dot_general(qall, kf, _NT, preferred_element_type=F32)
        sel = jnp.concatenate(
            [jnp.dot(jnp.where(blk == 0.0, 1.0, 0.0).astype(BF16), spread, preferred_element_type=F32)
             for blk in sel_blocks], axis=1)
        sel = jnp.concatenate([sel] * N_HEADS_B, axis=0)
        row_head = lax.broadcasted_iota(I32, s.shape, 0) >> 3
        col_head = lax.broadcasted_iota(I32, s.shape, 1) & (N_HEADS_B - 1)
        s = jnp.where((sel > 0.5) & (row_head == col_head), s, NEG_BIG)
        m_new = jnp.maximum(m_old, jnp.max(s, axis=1, keepdims=True))
        alpha = jnp.exp(m_old - m_new)
        p = jnp.exp(s - m_new)
        l_new = alpha * l_old + jnp.sum(p, axis=1, keepdims=True)
        pv = jnp.dot(p.astype(BF16), vf, preferred_element_type=F32)
        return m_new, l_new, acc * alpha + pv

    def group(g, carry):
        slot = g % 2
        for cp in group_copies(g, slot):
            cp.wait()

        @pl.when(g + 1 < n_groups)
        def _():
            for cp in group_copies(g + 1, 1 - slot):
                cp.start()
        return attend(kbuf[slot], vbuf[slot], [bias[g * grp + jj] for jj in range(grp)], carry)

    init = (jnp.full((64, 1), NEG_BIG, F32), jnp.zeros((64, 1), F32), jnp.zeros((64, HEAD_DIM_B), F32))
    carry = lax.fori_loop(0, n_groups, group, init)
    _, l_fin, acc = attend(knew_ref[...], vnew_ref[...], [bias[n_pages]], carry)
    o_ref[...] = acc / l_fin


def _dsa_sample(q, k_new, v_new, qi, wi, ki_new, pool_k, pool_v, pool_ki, page_table, layer):
    db, t, _ = q.shape
    n_pages = page_table.shape[1]
    page = pool_ki.shape[2]
    topk = min(TOPK_MAX, (n_pages * page + t) // 4)
    nj = n_pages + 1
    gk = SAMPLE_GROUP_PAGES * page

    def head_major(a):
        return jnp.pad(a.swapaxes(1, 2), ((0, 0), (0, 0), (0, 8 - t), (0, 0)))

    qih = head_major(qi.reshape(db, t, N_IDX_HEADS, IDX_DIM)).reshape(db, 64, IDX_DIM).astype(BF16)
    wib = jnp.broadcast_to(head_major(wi.reshape(db, t, N_IDX_HEADS, 1)).reshape(db, 64, 1), (db, 64, LANES))
    qh = head_major(q.reshape(db, t, N_HEADS_B, HEAD_DIM_B) * (HEAD_DIM_B ** -0.5))
    qh = qh.reshape(db, 64, HEAD_DIM_B).astype(BF16)
    assert n_pages % SAMPLE_GROUP_PAGES == 0
    pad_rows = lambda a: jnp.pad(a, ((0, 0), (0, page - t)) + ((0, 0),) * (a.ndim - 2))
    seq3 = lambda b, pt: (b, 0, 0)
    seq4 = lambda b, pt: (b, 0, 0, 0)
    hbm = pl.BlockSpec(memory_space=pl.ANY)
    kv_spec = pl.BlockSpec((None, page, N_HEADS_B, HEAD_DIM_B), seq4)
    o = pl.pallas_call(
        functools.partial(_dsa_sample_body, topk=topk, n_pages=n_pages, t_new=t, page=page, layer=layer),
        grid_spec=pltpu.PrefetchScalarGridSpec(
            num_scalar_prefetch=1, grid=(db,),
            in_specs=[pl.BlockSpec((None, 64, IDX_DIM), seq3), pl.BlockSpec((None, 64, LANES), seq3),
                      pl.BlockSpec((None, 64, HEAD_DIM_B), seq3),
                      pl.BlockSpec((None, page, IDX_DIM), seq3), kv_spec, kv_spec, hbm, hbm, hbm],
            out_specs=pl.BlockSpec((None, 64, HEAD_DIM_B), seq3),
            scratch_shapes=[pltpu.VMEM((n_pages * page, IDX_DIM), F32),
                            pltpu.VMEM((2, gk, N_HEADS_B, HEAD_DIM_B), F32),
                            pltpu.VMEM((2, gk, N_HEADS_B, HEAD_DIM_B), F32),
                            pltpu.VMEM((nj, 8, LANES), F32), pltpu.VMEM((nj, 8, LANES), F32),
                            pltpu.SemaphoreType.DMA((3,))]),
        out_shape=jax.ShapeDtypeStruct((db, 64, HEAD_DIM_B), F32),
        compiler_params=_params(1, 48), name="dsa_sample")(
            page_table, qih, wib, qh, pad_rows(ki_new), pad_rows(k_new), pad_rows(v_new),
            pool_ki, pool_k, pool_v)
    o = o.reshape(db, N_HEADS_B, 8, HEAD_DIM_B)[:, :, :t, :]
    return o.swapaxes(1, 2).reshape(db, t, N_HEADS_B * HEAD_DIM_B)


def _pool_out_body(u_ref, pre_ref, battn_ref, x_ref, wpool_ref, scale_ref, wout_ref,
                   x1_ref, hist_ref, ext, *, pos0):
    tt = u_ref.shape[0]
    ti = pl.program_id(1)
    h0 = POOL_HIST

    @pl.when(ti == 0)
    def _():
        ext[0:h0, :] = pre_ref[...]

    ext[h0:h0 + tt, :] = u_ref[...]
    pos = pos0 + ti * tt + lax.broadcasted_iota(I32, (tt, LANES), 0)
    outs = []
    for g, w in enumerate(POOL_WINDOWS):
        cols = slice(g * LANES, (g + 1) * LANES)
        cur = ext[h0:h0 + tt, cols]
        s = cur
        for back in range(1, w):
            s = s + ext[h0 - back:h0 - back + tt, cols]
        cnt = jnp.minimum(pos + 1, w).astype(F32)
        d = (s / cnt - cur).astype(BF16)
        outs.append(jnp.dot(d, wpool_ref[g], preferred_element_type=F32) * scale_ref[:, cols])
    a = jnp.concatenate(outs, axis=1).astype(BF16)
    y = jnp.dot(a, wout_ref[0:512, :], preferred_element_type=F32)
    y = y + jnp.dot(battn_ref[...].astype(BF16), wout_ref[512:1024, :], preferred_element_type=F32)
    x1_ref[...] = x_ref[...] + y
    new_hist = ext[tt:tt + h0, :]
    ext[0:h0, :] = new_hist
    hist_ref[...] = new_hist


def _pool_out(u, prefix, battn, x, w_pool, pool_scale, w_out, pos0):
    bsz, t, _ = u.shape
    d = x.shape[-1]
    tt = 256 if t % 256 == 0 else t
    pre = jnp.pad(prefix, ((0, 0), (POOL_HIST - prefix.shape[1], 0), (0, 0)))
    tile = lambda b, i: (b, i, 0)
    perb = lambda b, i: (b, 0, 0)
    x1, hist = pl.pallas_call(
        functools.partial(_pool_out_body, pos0=pos0),
        grid=(bsz, t // tt),
        in_specs=[pl.BlockSpec((None, tt, 512), tile), pl.BlockSpec((None, POOL_HIST, 512), perb),
                  pl.BlockSpec((None, tt, 512), tile), pl.BlockSpec((None, tt, d), tile),
                  pl.BlockSpec((4, LANES, LANES), lambda b, i: (0, 0, 0)),
                  pl.BlockSpec((1, 512), lambda b, i: (0, 0)),
                  pl.BlockSpec((1024, d), lambda b, i: (0, 0))],
        out_specs=[pl.BlockSpec((None, tt, d), tile), pl.BlockSpec((None, POOL_HIST, 512), perb)],
        out_shape=[jax.ShapeDtypeStruct((bsz, t, d), F32), jax.ShapeDtypeStruct((bsz, POOL_HIST, 512), F32)],
        scratch_shapes=[pltpu.VMEM((POOL_HIST + tt, 512), F32)],
        compiler_params=_params(2, 32), name="pool_outproj")(
            u, pre, battn, x, w_pool.astype(BF16), pool_scale.reshape(1, 512), w_out.astype(BF16))
    return x1, hist[:, 1:, :]


def _conv_body(x_ref, pre_ref, g_ref, win_ref, bin_ref, wdw_ref, bdw_ref, lng_ref, lnb_ref, wout_ref,
               x1_ref, hist_ref, ext, ybuf, phase):
    tt, d = x_ref.shape
    ti = pl.program_id(1)
    h0 = CONV_HIST

    @pl.when(ti == 0)
    def _():
        ext[0:h0, :] = pre_ref[...]

    x = x_ref[...]
    hb = _rms(x, g_ref[...]).astype(BF16)
    a = jnp.dot(hb, win_ref[...], preferred_element_type=F32) + bin_ref[...]
    ext[h0:h0 + tt, :] = a[:, :d] * _sigmoid(a[:, d:])
    rc = min(tt, 128)
    first = h0 - (CONV_WIDTH - 1)
    for r0 in range(0, tt, rc):
        for c0 in range(0, d, LANES):
            cols = slice(c0, c0 + LANES)
            y = jnp.broadcast_to(bdw_ref[:, cols], (rc, LANES))
            for r in range(min(8, CONV_WIDTH)):
                taps = range(r, CONV_WIDTH, 8)
                span = rc + taps[-1] - r
                ph = phase.at[r]
                ph[0:span, :] = ext[first + r0 + r:first + r0 + r + span, cols]
                for j in taps:
                    y = y + ph[j - r:j - r + rc, :] * wdw_ref[j:j + 1, cols]
            ybuf[r0:r0 + rc, cols] = y
    y = ybuf[...]
    mu = jnp.mean(y, axis=-1, keepdims=True)
    yc = y - mu
    var = jnp.mean(yc * yc, axis=-1, keepdims=True)
    yn = yc * lax.rsqrt(var + EPS) * lng_ref[...] + lnb_ref[...]
    act = (yn * _sigmoid(yn)).astype(BF16)
    x1_ref[...] = x + jnp.dot(act, wout_ref[...], preferred_element_type=F32)
    new_hist = ext[tt:tt + h0, :]
    ext[0:h0, :] = new_hist
    hist_ref[...] = new_hist


def _conv_mixer(x, prefix, g, w_in, b_in, w_dw, b_dw, ln_g, ln_b, w_out):
    bsz, t, d = x.shape
    tt = 256 if t % 256 == 0 else t
    pre = jnp.pad(prefix, ((0, 0), (CONV_HIST - prefix.shape[1], 0), (0, 0)))
    wdw = jnp.pad(w_dw, ((0, CONV_HIST - CONV_WIDTH), (0, 0)))
    tile = lambda b, i: (b, i, 0)
    perb = lambda b, i: (b, 0, 0)
    fixed = lambda b, i: (0, 0)
    vec = lambda n: pl.BlockSpec((1, n), fixed)
    x1, hist = pl.pallas_call(
        _conv_body,
        grid=(bsz, t // tt),
        in_specs=[pl.BlockSpec((None, tt, d), tile), pl.BlockSpec((None, CONV_HIST, d), perb), vec(d),
                  pl.BlockSpec((d, 2 * d), fixed), vec(2 * d), pl.BlockSpec((CONV_HIST, d), fixed), vec(d),
                  vec(d), vec(d), pl.BlockSpec((d, d), fixed)],
        out_specs=[pl.BlockSpec((None, tt, d), tile), pl.BlockSpec((None, CONV_HIST, d), perb)],
        out_shape=[jax.ShapeDtypeStruct((bsz, t, d), F32), jax.ShapeDtypeStruct((bsz, CONV_HIST, d), F32)],
        scratch_shapes=[pltpu.VMEM((CONV_HIST + tt, d), F32), pltpu.VMEM((tt, d), F32),
                        pltpu.VMEM((8, min(tt, 128) + CONV_HIST, LANES), F32)],
        compiler_params=_params(2, 40), name="conv_mixer")(
            x, pre, g.reshape(1, d), w_in.astype(BF16), b_in.reshape(1, 2 * d), wdw, b_dw.reshape(1, d),
            ln_g.reshape(1, d), ln_b.reshape(1, d), w_out.astype(BF16))
    return x1, hist[:, CONV_HIST - (CONV_WIDTH - 1):, :]


def _memkv_body(mem_ref, g_ref, wk_ref, wv_ref, k_ref, v_ref, kflat_ref, vflat_ref):
    mb = _rms(mem_ref[...], g_ref[...]).astype(BF16)
    k = jnp.dot(mb, wk_ref[...], preferred_element_type=F32)
    v = jnp.dot(mb, wv_ref[...], preferred_element_type=F32)
    kflat_ref[...] = k
    vflat_ref[...] = v
    hd = k_ref.shape[-1]
    for h in range(N_HEADS_X):
        k_ref[:, h, :] = k[:, h * hd:(h + 1) * hd]
        v_ref[:, h, :] = v[:, h * hd:(h + 1) * hd]


def _mem_kv(mem, g_mem, w_ck, w_cv):
    bsz, m, d = mem.shape
    depth = g_mem.shape[0]
    hd = d // N_HEADS_X
    out = jax.ShapeDtypeStruct((depth, bsz, m, N_HEADS_X, hd), F32)
    flat = jax.ShapeDtypeStruct((depth, bsz, m, d), F32)
    wspec = pl.BlockSpec((None, d, d), lambda l, b: (l, 0, 0))
    ospec = pl.BlockSpec((None, None, m, N_HEADS_X, hd), lambda l, b: (l, b, 0, 0, 0))
    fspec = pl.BlockSpec((None, None, m, d), lambda l, b: (l, b, 0, 0))
    return pl.pallas_call(
        _memkv_body, grid=(depth, bsz),
        in_specs=[pl.BlockSpec((None, m, d), lambda l, b: (b, 0, 0)),
                  pl.BlockSpec((None, 1, d), lambda l, b: (l, 0, 0)), wspec, wspec],
        out_specs=[ospec, ospec, fspec, fspec], out_shape=[out, out, flat, flat],
        compiler_params=_params(2, 32), name="mem_kv")(
            mem, g_mem.reshape(depth, 1, d), w_ck.astype(BF16), w_cv.astype(BF16))


def _cross_body(x_ref, g_ref, wq_ref, mk_ref, mv_ref, wo_ref, o_ref):
    x = x_ref[...]
    d = x.shape[-1]
    hd = d // N_HEADS_X
    hb = _rms(x, g_ref[...]).astype(BF16)
    q = (jnp.dot(hb, wq_ref[...], preferred_element_type=F32) * (hd ** -0.5)).astype(BF16)
    outs = []
    for h in range(N_HEADS_X):
        cols = slice(h * hd, (h + 1) * hd)
        head = (slice(None), cols) if len(mk_ref.shape) == 2 else (slice(None), h, slice(None))
        s = lax.dot_general(q[:, cols], mk_ref[head].astype(BF16), _NT, preferred_element_type=F32)
        p = jnp.exp(s - jnp.max(s, axis=-1, keepdims=True))
        l = jnp.sum(p, axis=-1, keepdims=True)
        o = jnp.dot(p.astype(BF16), mv_ref[head].astype(BF16), preferred_element_type=F32)
        outs.append(o / l)
    o = jnp.concatenate(outs, axis=1).astype(BF16)
    o_ref[...] = x + jnp.dot(o, wo_ref[...], preferred_element_type=F32)


def _cross_attn(x, g, w_cq, mem_k, mem_v, w_co, layer):
    bsz, t, d = x.shape
    tt = 256 if t % 256 == 0 else t
    tile = lambda b, i: (b, i, 0)
    fixed = lambda b, i: (0, 0)
    mem_spec = pl.BlockSpec((None, None) + mem_k.shape[2:], lambda b, i: (layer, b) + (0,) * (mem_k.ndim - 2))
    return pl.pallas_call(
        _cross_body, grid=(bsz, t // tt),
        in_specs=[pl.BlockSpec((None, tt, d), tile), pl.BlockSpec((1, d), fixed), pl.BlockSpec((d, d), fixed),
                  mem_spec, mem_spec, pl.BlockSpec((d, d), fixed)],
        out_specs=pl.BlockSpec((None, tt, d), tile),
        out_shape=jax.ShapeDtypeStruct((bsz, t, d), F32),
        compiler_params=_params(2, 32), name="cross_attn")(
            x, g.reshape(1, d), w_cq.astype(BF16), mem_k, mem_v, w_co.astype(BF16))


def _store_row_tiles(ref, x):
    for s in range(ref.shape[1]):
        ref[:, s, :] = x[:, s * LANES:(s + 1) * LANES]


def _load_row_tiles(ref):
    return jnp.concatenate([ref[:, s, :] for s in range(ref.shape[1])], axis=1)


def _router_body(x_ref, g_ref, wr_ref, br_ref, h_ref, eid_ref, gate_ref, rank_ref, cnt_ref, run):
    tm = x_ref.shape[0]
    step = pl.program_id(0)

    @pl.when(step == 0)
    def _():
        run[...] = jnp.zeros(run.shape, F32)

    h = _rms(x_ref[...], g_ref[...])
    _store_row_tiles(h_ref, h)
    lg = jnp.dot(h, wr_ref[...], preferred_element_type=F32, precision=lax.Precision.HIGHEST) + br_ref[...]
    lane = lax.broadcasted_iota(I32, (tm, LANES), 1).astype(F32)
    neg = -jnp.inf

    def first_argmax(v):
        mx = jnp.max(v, axis=1, keepdims=True)
        return mx, jnp.min(jnp.where(v == mx, lane, 1e9), axis=1, keepdims=True)

    is_g = lane < N_GROUPS
    gmax, g_sel = first_argmax(jnp.where(is_g, lg, neg))
    g_prob = 1.0 / jnp.sum(jnp.where(is_g, jnp.exp(lg - gmax), 0.0), axis=1, keepdims=True)
    lo = N_GROUPS + g_sel * EXPERTS_PER_GROUP
    el = jnp.where((lane >= lo) & (lane < lo + EXPERTS_PER_GROUP), lg, neg)
    v1, i1 = first_argmax(el)
    v2, i2 = first_argmax(jnp.where(lane == i1, neg, el))
    e2 = jnp.exp(v2 - v1)
    gate1 = g_prob / (1.0 + e2)
    gate2 = g_prob * e2 / (1.0 + e2)
    id1 = i1 - N_GROUPS
    id2 = i2 - N_GROUPS
    oh1 = jnp.where(lane == id1, 1.0, 0.0)
    oh2 = jnp.where(lane == id2, 1.0, 0.0)
    r = lax.broadcasted_iota(I32, (tm, tm), 0)
    c = lax.broadcasted_iota(I32, (tm, tm), 1)
    strict = jnp.where(c < r, 1.0, 0.0).astype(BF16)
    before = jnp.dot(strict, (oh1 + oh2).astype(BF16), preferred_element_type=F32) + run[...]
    rank1 = jnp.sum(oh1 * before, axis=1, keepdims=True)
    rank2 = jnp.sum(oh2 * before, axis=1, keepdims=True)
    run[...] = run[...] + jnp.sum(oh1 + oh2, axis=0, keepdims=True)
    two = lax.broadcasted_iota(I32, (tm, 2), 1)
    eid_ref[...] = jnp.where(two == 0, id1, id2).astype(I32)
    gate_ref[...] = jnp.where(two == 0, gate1, gate2)
    rank_ref[...] = jnp.where(two == 0, rank1, rank2).astype(I32)
    cnt_ref[...] = run[...].astype(I32)


def _router(x2d, g, w_rg, b_rg, w_re, b_re):
    n, d = x2d.shape
    tm = _token_tile(n)
    w_r = jnp.concatenate([w_rg, w_re.transpose(1, 0, 2).reshape(d, N_EXPERTS)], axis=1)
    w_r = jnp.pad(w_r, ((0, 0), (0, LANES - w_r.shape[1])))
    b_r = jnp.pad(jnp.concatenate([b_rg, b_re.reshape(N_EXPERTS)]), (0, LANES - N_GROUPS - N_EXPERTS))
    row = lambda i: (i, 0)
    fixed = lambda i: (0, 0)
    pair = pl.BlockSpec((tm, 2), row)
    return pl.pallas_call(
        _router_body, grid=(n // tm,),
        in_specs=[pl.BlockSpec((tm, d), row), pl.BlockSpec((1, d), fixed), pl.BlockSpec((d, LANES), fixed),
                  pl.BlockSpec((1, LANES), fixed)],
        out_specs=[pl.BlockSpec((tm, d // LANES, LANES), lambda i: (i, 0, 0)), pair, pair, pair,
                   pl.BlockSpec((1, LANES), fixed)],
        out_shape=[jax.ShapeDtypeStruct((n, d // LANES, LANES), F32), jax.ShapeDtypeStruct((n, 2), I32),
                   jax.ShapeDtypeStruct((n, 2), F32), jax.ShapeDtypeStruct((n, 2), I32),
                   jax.ShapeDtypeStruct((1, LANES), I32)],
        scratch_shapes=[pltpu.VMEM((1, LANES), F32)],
        compiler_params=_params(1, 32), name="moe_router")(x2d, g.reshape(1, d), w_r, b_r.reshape(1, LANES))


def _dispatch_body(slot_ref, pend_ref, h_ref, xs_hbm, zeros, sem, *, tm, tb, nb):
    step = pl.program_id(0)

    @pl.when(step == 0)
    def _():
        zeros[...] = jnp.zeros(zeros.shape, F32)
        n_used = pend_ref[N_EXPERTS - 1] // tb

        def zero_copy(blk):
            return pltpu.make_async_copy(zeros, xs_hbm.at[pl.ds(pl.multiple_of(blk * tb, tb), tb)], sem.at[0])

        def seg_last_block(e):
            start = jnp.where(e == 0, 0, pend_ref[jnp.maximum(e - 1, 0)])
            return pend_ref[e] > start, jnp.maximum(pend_ref[e] // tb - 1, 0)

        def each(fn):
            def seg(e, carry):
                nonempty, blk = seg_last_block(e)

                @pl.when(nonempty)
                def _():
                    fn(zero_copy(blk))
                return carry
            lax.fori_loop(0, N_EXPERTS, seg, 0)

            def tail(blk, carry):
                @pl.when(blk >= n_used)
                def _():
                    fn(zero_copy(blk))
                return carry
            lax.fori_loop(0, nb, tail, 0)

        each(lambda cp: cp.start())
        each(lambda cp: cp.wait())

    base = step * tm

    def row_copy(r, j):
        return pltpu.make_async_copy(h_ref.at[r], xs_hbm.at[slot_ref[2 * (base + r) + j]], sem.at[1])

    def rstart(r, carry):
        row_copy(r, 0).start()
        row_copy(r, 1).start()
        return carry
    lax.fori_loop(0, tm, rstart, 0)

    def rwait(r, carry):
        row_copy(r, 0).wait()
        row_copy(r, 1).wait()
        return carry
    lax.fori_loop(0, tm, rwait, 0)


def _dispatch(h2, slot_flat, pend, n_rows, tb):
    n, s, l = h2.shape
    tm = _token_tile(n)
    return pl.pallas_call(
        functools.partial(_dispatch_body, tm=tm, tb=tb, nb=n_rows // tb),
        grid_spec=pltpu.PrefetchScalarGridSpec(
            num_scalar_prefetch=2, grid=(n // tm,),
            in_specs=[pl.BlockSpec((tm, s, l), lambda i, sl, pe: (i, 0, 0))],
            out_specs=pl.BlockSpec(memory_space=pl.ANY),
            scratch_shapes=[pltpu.VMEM((tb, s, l), F32), pltpu.SemaphoreType.DMA((2,))]),
        out_shape=jax.ShapeDtypeStruct((n_rows, s, l), F32),
        compiler_params=_params(1), name="moe_dispatch")(slot_flat, pend, h2)


def _expert_body(bexp_ref, nvalid_ref, xs_ref, w1_ref, w3_ref, w2_ref, ys_ref, w1b, w3b, w2b):
    i = pl.program_id(0)
    live = i < nvalid_ref[0]
    new_expert = (i == 0) | (bexp_ref[i] != bexp_ref[jnp.maximum(i - 1, 0)])

    @pl.when(live & new_expert)
    def _():
        w1b[...] = w1_ref[...].astype(BF16)
        w3b[...] = w3_ref[...].astype(BF16)
        w2b[...] = w2_ref[...].astype(BF16)

    @pl.when(live)
    def _():
        xb = _load_row_tiles(xs_ref).astype(BF16)
        a = jnp.dot(xb, w1b[...], preferred_element_type=F32)
        b = jnp.dot(xb, w3b[...], preferred_element_type=F32)
        hmid = (a * _sigmoid(a) * b).astype(BF16)
        _store_row_tiles(ys_ref, jnp.dot(hmid, w2b[...], preferred_element_type=F32))

    @pl.when(i >= nvalid_ref[0])
    def _():
        ys_ref[...] = jnp.zeros(ys_ref.shape, F32)


def _experts(xs, blk_exp, nvalid, w1, w3, w2, layer, tb):
    p, s, l = xs.shape
    d = s * l
    ff = w1.shape[-1]
    nb = p // tb
    rows_in = lambda i, be, nv: (jnp.minimum(i, jnp.maximum(nv[0] - 1, 0)), 0, 0)
    wmap = lambda i, be, nv: (layer, be[i], 0, 0)
    return pl.pallas_call(
        _expert_body,
        grid_spec=pltpu.PrefetchScalarGridSpec(
            num_scalar_prefetch=2, grid=(nb,),
            in_specs=[pl.BlockSpec((tb, s, l), rows_in), pl.BlockSpec((None, None, d, ff), wmap),
                      pl.BlockSpec((None, None, d, ff), wmap), pl.BlockSpec((None, None, ff, d), wmap)],
            out_specs=pl.BlockSpec((tb, s, l), lambda i, be, nv: (i, 0, 0)),
            scratch_shapes=[pltpu.VMEM((d, ff), BF16), pltpu.VMEM((d, ff), BF16), pltpu.VMEM((ff, d), BF16)]),
        out_shape=jax.ShapeDtypeStruct((p, s, l), F32),
        compiler_params=_params(1, 40), name="moe_experts")(blk_exp, nvalid, xs, w1, w3, w2)


def _combine_body(slot_ref, x_ref, gate_ref, gf_ref, ys_hbm, o_ref, buf, sem, *, final_norm):
    tm = x_ref.shape[0]
    base = pl.program_id(0) * tm * 2

    def row_copy(r, j):
        return pltpu.make_async_copy(ys_hbm.at[slot_ref[base + 2 * r + j]], buf.at[j, r], sem.at[0])

    def start(r, carry):
        row_copy(r, 0).start()
        row_copy(r, 1).start()
        return carry
    lax.fori_loop(0, tm, start, 0)

    def wait(r, carry):
        row_copy(r, 0).wait()
        row_copy(r, 1).wait()
        return carry
    lax.fori_loop(0, tm, wait, 0)

    gate = gate_ref[...]
    y = x_ref[...] + _load_row_tiles(buf.at[0]) * gate[:, 0:1] + _load_row_tiles(buf.at[1]) * gate[:, 1:2]
    if final_norm:
        y = _rms(y, gf_ref[...])
    o_ref[...] = y


def _combine(x2d, gates, slot_flat, ys, g_final):
    n, d = x2d.shape
    tm = 256 if n % 256 == 0 else n
    final_norm = g_final is not None
    gf = (g_final if final_norm else jnp.ones((d,), F32)).reshape(1, d)
    row = lambda i, s: (i, 0)
    return pl.pallas_call(
        functools.partial(_combine_body, final_norm=final_norm),
        grid_spec=pltpu.PrefetchScalarGridSpec(
            num_scalar_prefetch=1, grid=(n // tm,),
            in_specs=[pl.BlockSpec((tm, d), row), pl.BlockSpec((tm, 2), row),
                      pl.BlockSpec((1, d), lambda i, s: (0, 0)), pl.BlockSpec(memory_space=pl.ANY)],
            out_specs=pl.BlockSpec((tm, d), row),
            scratch_shapes=[pltpu.VMEM((2, tm, d // LANES, LANES), F32), pltpu.SemaphoreType.DMA((1,))]),
        out_shape=jax.ShapeDtypeStruct((n, d), F32),
        compiler_params=_params(1, 32), name="moe_combine")(slot_flat, x2d, gates, gf, ys)


def _moe(x2d, g, w_rg, b_rg, w_re, b_re, w1, w3, w2, layer, g_final):
    n, _ = x2d.shape
    tb = 256 if n >= 2048 else 128
    h2, eid, gates, rank, counts = _router(x2d, g, w_rg, b_rg, w_re, b_re)
    counts = counts[0, :N_EXPERTS]
    pcounts = (counts + tb - 1) // tb * tb
    pend = jnp.cumsum(pcounts)
    pstart = pend - pcounts
    slot_flat = (pstart[eid] + rank).reshape(2 * n).astype(I32)
    nb = -(-(2 * n + N_EXPERTS * (tb - 1)) // tb)
    blk_start = jnp.arange(nb, dtype=I32) * tb
    blk_exp = jnp.minimum(jnp.sum(pend[None, :] <= blk_start[:, None], axis=1), N_EXPERTS - 1).astype(I32)
    nvalid = (pend[-1:] // tb).astype(I32)
    xs = _dispatch(h2, slot_flat, pend.astype(I32), nb * tb, tb)
    ys = _experts(xs, blk_exp, nvalid, w1, w3, w2, layer, tb)
    return _combine(x2d, gates, slot_flat, ys, g_final)


def _trunk(x, pos0, pool_prefix, conv_prefix, mem_k, mem_v, attn_fn, p, we):
    bsz, t, d = x.shape
    n = bsz * t
    proj = _inproj(x.reshape(n, d), p["g_mix"][0], p["w_in_even"][0], with_vt=attn_fn == "prompt")
    u, qb, k, v, kb, qib, ki, kib, wi = proj[:9]
    r3 = lambda a: a.reshape(bsz, t, a.shape[-1])
    if attn_fn == "prompt":
        vt = proj[9].reshape(bsz, t // KEY_CHUNK, 512, KEY_CHUNK)
        battn = _dsa_prompt(r3(qib), r3(wi), r3(kib), r3(qb), r3(kb), vt)
    else:
        r4 = lambda a: a.reshape(bsz, t, N_HEADS_B, HEAD_DIM_B)
        battn = attn_fn(r3(qb).astype(F32), r4(k), r4(v), r3(qib).astype(F32), r3(wi), r3(ki))
    x, new_pool = _pool_out(r3(u), pool_prefix[0], battn, x, p["w_pool"][0], p["pool_scale"][0],
                            p["w_out_even"][0], pos0)
    x = _cross_attn(x, p["g_cross"][0], p["w_cq"][0], mem_k, mem_v, p["w_co"][0], 0)
    x = _moe(x.reshape(n, d), p["g_ffn"][0], p["w_rg"][0], p["b_rg"][0], p["w_re"][0], p["b_re"][0],
             we[0], we[1], we[2], 0, None).reshape(bsz, t, d)
    x, new_conv = _conv_mixer(x, conv_prefix[0], p["g_mix"][1], p["w_conv_in"][0], p["b_conv_in"][0],
                              p["w_dw"][0], p["b_dw"][0], p["ln_g"][0], p["ln_b"][0], p["w_conv_out"][0])
    x = _cross_attn(x, p["g_cross"][1], p["w_cq"][1], mem_k, mem_v, p["w_co"][1], 1)
    y = _moe(x.reshape(n, d), p["g_ffn"][1], p["w_rg"][1], p["b_rg"][1], p["w_re"][1], p["b_re"][1],
             we[0], we[1], we[2], 1, p["g_final"]).reshape(bsz, t, d)
    hk = lambda a: a.reshape(1, bsz, t, N_HEADS_B, HEAD_DIM_B)
    return y, hk(k), hk(v), ki.reshape(1, bsz, t, IDX_DIM), new_pool[None], new_conv[None]


def kernel(x_prompt, x_sample, mem_prompt, cache_attn_k, cache_attn_v, cache_idx_k, page_table,
           state_pool, state_conv, cache_mem_k, cache_mem_v, g_mix, g_cross, g_mem, g_ffn, g_final,
           w_in_even, w_pool, pool_scale, w_out_even, w_conv_in, b_conv_in, w_dw, b_dw, ln_g, ln_b,
           w_conv_out, w_cq, w_ck, w_cv, w_co, w_rg, b_rg, w_re, b_re, w_e1, w_e3, w_e2):
    p = {"g_mix": g_mix, "g_cross": g_cross, "g_ffn": g_ffn, "g_final": g_final,
         "w_in_even": w_in_even, "w_pool": w_pool, "pool_scale": pool_scale, "w_out_even": w_out_even,
         "w_conv_in": w_conv_in, "b_conv_in": b_conv_in, "w_dw": w_dw, "b_dw": b_dw, "ln_g": ln_g,
         "ln_b": ln_b, "w_conv_out": w_conv_out, "w_cq": w_cq, "w_co": w_co, "w_rg": w_rg, "b_rg": b_rg,
         "w_re": w_re, "b_re": b_re}
    we = (w_e1, w_e3, w_e2)
    bsz, _, d = x_prompt.shape

    def sample_attn(q, k, v, qi, wi, ki):
        return _dsa_sample(q, k, v, qi, wi, ki, cache_attn_k, cache_attn_v, cache_idx_k, page_table, 0)

    past = page_table.shape[1] * cache_idx_k.shape[2]
    y_s, nk_s, nv_s, nki_s, npool_s, nconv_s = _trunk(
        x_sample, past, state_pool, state_conv, cache_mem_k, cache_mem_v, sample_attn, p, we)

    mem_k_p, mem_v_p, mem_k_flat, mem_v_flat = _mem_kv(mem_prompt, g_mem, w_ck, w_cv)
    zeros_pool = jnp.zeros((1, bsz, POOL_HIST - 1, 512), F32)
    zeros_conv = jnp.zeros((1, bsz, CONV_WIDTH - 1, d), F32)
    y_p, nk_p, nv_p, nki_p, npool_p, nconv_p = _trunk(
        x_prompt, 0, zeros_pool, zeros_conv, mem_k_flat, mem_v_flat, "prompt", p, we)

    return (y_p, y_s, nk_p, nv_p, nki_p, npool_p, nconv_p, mem_k_p, mem_v_p,
            nk_s, nv_s, nki_s, npool_s, nconv_s)
```
